```python
import functools
import jax, jax.numpy as jnp
from jax import lax
import numpy as np

D_MODEL = 1024
BATCH = 4
SEQ = 4096
DEPTH = 2
DEC_BATCH = 32
DEC_SEQ = 1
PAST_LEN = 8192
PAGE_SIZE = 128

HEAD_DIM = 64
A_W = D_MODEL // 4
B_W = D_MODEL // 4
C_W = D_MODEL // 2
H_A = A_W // HEAD_DIM
H_C = C_W // HEAD_DIM
IN_W = 2 * A_W + 2 * B_W + 3 * C_W + H_C
CHUNK = 128
CONV_W = 31
FFN_CONV_W = 3
D_FF = ((8 * D_MODEL // 3 + 127) // 128) * 128
Q_BLOCK = 128
SCALE = HEAD_DIM ** -0.5
EPS = 1e-6
FORGET_BIAS = 4.0

kernel_name = 'hybrid_gmlp_conformer_fox_decoder_step'


def _rms(x):
    x32 = x.astype(jnp.float32)
    y = x32 * lax.rsqrt(jnp.mean(x32 * x32, axis=-1, keepdims=True) + EPS)
    return y.astype(x.dtype)


def rmsnorm(x, g):
    return _rms(x) * g


def layernorm(x, g, b):
    x32 = x.astype(jnp.float32)
    xc = x32 - jnp.mean(x32, axis=-1, keepdims=True)
    y = xc * lax.rsqrt(jnp.mean(xc * xc, axis=-1, keepdims=True) + EPS)
    return y.astype(x.dtype) * g + b


def causal_dwconv(x, buf, w, b):
    width = w.shape[0]
    xp = jnp.concatenate([buf.astype(x.dtype), x], axis=1)
    y = lax.conv_general_dilated(xp, w[:, None, :].astype(x.dtype), window_strides=(1,), padding='VALID',
                                 dimension_numbers=('NWC', 'WIO', 'NWC'), feature_group_count=x.shape[-1])
    return y + b, xp[:, xp.shape[1] - (width - 1):]


def chunk_spatial_gating(u, v, ln_g, ln_b, w_s, b_s):
    nb, t, _ = v.shape
    v = layernorm(v, ln_g, ln_b)
    n_chunks = -(-t // CHUNK)
    vp = jnp.pad(v, ((0, 0), (0, n_chunks * CHUNK - t), (0, 0))).reshape(nb, n_chunks, CHUNK, H_A, HEAD_DIM)
    causal = jnp.tril(jnp.ones((CHUNK, CHUNK), dtype=bool))
    w_masked = jnp.where(causal[None], w_s, 0.0)
    sv = jnp.einsum('hts,bnshd->bnthd', w_masked, vp) + b_s.T[:, :, None]
    sv = sv.reshape(nb, n_chunks * CHUNK, A_W)[:, :t]
    return u * sv, v


def conformer_conv(a, gate, buf, conv_w, conv_b, ln_g, ln_b):
    glu = a * jax.nn.sigmoid(gate)
    y, new_buf = causal_dwconv(glu, buf, conv_w, conv_b)
    return jax.nn.silu(layernorm(y, ln_g, ln_b)), new_buf


def forget_attention_prompt(q, k, v, logf):
    nb, t, h, dh = q.shape
    n_blocks = t // Q_BLOCK
    cum = jnp.cumsum(logf, axis=1).transpose(0, 2, 1)
    q_blocks = q.reshape(nb, n_blocks, Q_BLOCK, h, dh).transpose(1, 0, 2, 3, 4)
    c_blocks = cum.reshape(nb, h, n_blocks, Q_BLOCK).transpose(2, 0, 1, 3)
    k_pos = jnp.arange(t)

    def block(args):
        i, qi, ci = args
        s = jnp.einsum('bqhd,bkhd->bhqk', qi, k, preferred_element_type=jnp.float32) * SCALE
        s = s + ci[..., None] - cum[:, :, None, :]
        q_pos = i * Q_BLOCK + jnp.arange(Q_BLOCK)
        s = jnp.where(k_pos[None, :] <= q_pos[:, None], s, -jnp.inf)
        p = jax.nn.softmax(s, axis=-1)
        return jnp.einsum('bhqk,bkhd->bqhd', p.astype(v.dtype), v)

    out = lax.map(block, (jnp.arange(n_blocks), q_blocks, c_blocks))
    return out.transpose(1, 0, 2, 3, 4).reshape(nb, t, h, dh)


def forget_attention_sample(q, k, v, logf, past_k, past_v, past_logf):
    p_len = past_k.shape[1]
    t = q.shape[1]
    keys = jnp.concatenate([past_k.astype(k.dtype), k], axis=1)
    vals = jnp.concatenate([past_v.astype(v.dtype), v], axis=1)
    cum = jnp.cumsum(jnp.concatenate([past_logf.astype(jnp.float32), logf], axis=1), axis=1).transpose(0, 2, 1)
    s = jnp.einsum('bqhd,bkhd->bhqk', q, keys, preferred_element_type=jnp.float32) * SCALE
    s = s + cum[:, :, p_len:, None] - cum[:, :, None, :]
    k_pos = jnp.arange(p_len + t)
    q_pos = p_len + jnp.arange(t)
    s = jnp.where(k_pos[None, :] <= q_pos[:, None], s, -jnp.inf)
    p = jax.nn.softmax(s, axis=-1)
    return jnp.einsum('bhqk,bkhd->bqhd', p.astype(vals.dtype), vals)


def trunk_layer(x, c, conv_buf, ffn_buf, attend, norm1_g, norm2_g, w_ada, b_ada, w_in, b_forget,
                a_ln_g, a_ln_b, w_s, b_s, conv_w, conv_b, conv_ln_g, conv_ln_b, mix_g, w_out,
                w_up, ffn_conv_w, ffn_conv_b, w_down):
    nb, t, _ = x.shape
    mod = (jax.nn.silu(c) @ w_ada + b_ada)[:, None, :]
    shift1, scale1, gate1, shift2, scale2, gate2 = jnp.split(mod, 6, axis=-1)
    h = rmsnorm(x, norm1_g) * (1 + scale1) + shift1
    z = h @ w_in
    za = jax.nn.gelu(z[..., :2 * A_W], approximate=False)
    y_a, chunk_v = chunk_spatial_gating(za[..., :A_W], za[..., A_W:], a_ln_g, a_ln_b, w_s, b_s)
    o = 2 * A_W
    y_b, new_conv = conformer_conv(z[..., o:o + B_W], z[..., o + B_W:o + 2 * B_W], conv_buf,
                                   conv_w, conv_b, conv_ln_g, conv_ln_b)
    o += 2 * B_W
    q = z[..., o:o + C_W].reshape(nb, t, H_C, HEAD_DIM)
    k = z[..., o + C_W:o + 2 * C_W].reshape(nb, t, H_C, HEAD_DIM)
    v = z[..., o + 2 * C_W:o + 3 * C_W].reshape(nb, t, H_C, HEAD_DIM)
    logf = jax.nn.log_sigmoid(z[..., o + 3 * C_W:].astype(jnp.float32) + b_forget.astype(jnp.float32))
    y_c = attend(q, k, v, logf).reshape(nb, t, C_W)
    y = jnp.concatenate([_rms(y_a), _rms(y_b), _rms(y_c)], axis=-1) * mix_g
    x = x + gate1 * (y @ w_out)
    h = rmsnorm(x, norm2_g) * (1 + scale2) + shift2
    up, new_ffn = causal_dwconv(h @ w_up, ffn_buf, ffn_conv_w, ffn_conv_b)
    x = x + gate2 * ((jax.nn.silu(up[..., :D_FF]) * up[..., D_FF:]) @ w_down)
    return x, (k, v, logf, new_conv, new_ffn, chunk_v)


def setup_inputs(seed: int = 0) -> dict:
    key = jax.random.key(seed)
    ks = iter(jax.random.split(key, 40))

    def nrm(shape, s):
        return s * jax.random.normal(next(ks), shape, jnp.float32)

    n_pages = PAST_LEN // PAGE_SIZE
    n_phys = (DEC_BATCH * n_pages * 5) // 4
    d_in = D_MODEL ** -0.5
    x_prompt = nrm((BATCH, SEQ, D_MODEL), 1.0)
    x_sample = nrm((DEC_BATCH, DEC_SEQ, D_MODEL), 1.0)
    cache_k = nrm((DEPTH, n_phys, PAGE_SIZE, H_C, HEAD_DIM), 1.0)
    cache_v = nrm((DEPTH, n_phys, PAGE_SIZE, H_C, HEAD_DIM), 1.0)
    cache_logf = jax.nn.log_sigmoid(FORGET_BIAS + nrm((DEPTH, n_phys, PAGE_SIZE, H_C), 1.0))
    state_conv = nrm((DEPTH, DEC_BATCH, CONV_W - 1, B_W), 0.5)
    state_ffn_conv = nrm((DEPTH, DEC_BATCH, FFN_CONV_W - 1, 2 * D_FF), 1.0)
    page_table = jax.random.permutation(next(ks), n_phys)[:DEC_BATCH * n_pages].reshape(DEC_BATCH, n_pages).astype(jnp.int32)
    c_prompt = nrm((BATCH, D_MODEL), 1.0)
    c_sample = nrm((DEC_BATCH, D_MODEL), 1.0)
    return {
        'x_prompt': x_prompt, 'x_sample': x_sample,
        'cache_k': cache_k, 'cache_v': cache_v, 'cache_logf': cache_logf,
        'state_conv': state_conv, 'state_ffn_conv': state_ffn_conv,
        'page_table': page_table, 'c_prompt': c_prompt, 'c_sample': c_sample,
        'norm1_g': 1.0 + nrm((DEPTH, D_MODEL), 0.05),
        'norm2_g': 1.0 + nrm((DEPTH, D_MODEL), 0.05),
        'w_ada': nrm((DEPTH, D_MODEL, 6 * D_MODEL), 0.5 * d_in),
        'b_ada': nrm((DEPTH, 6 * D_MODEL), 0.02),
        'w_in': nrm((DEPTH, D_MODEL, IN_W), d_in),
        'b_forget': FORGET_BIAS + nrm((DEPTH, H_C), 0.5),
        'a_ln_g': 1.0 + nrm((DEPTH, A_W), 0.05),
        'a_ln_b': nrm((DEPTH, A_W), 0.02),
        'w_s': nrm((DEPTH, H_A, CHUNK, CHUNK), CHUNK ** -0.5),
        'b_s': 1.0 + nrm((DEPTH, H_A, CHUNK), 0.1),
        'conv_w': nrm((DEPTH, CONV_W, B_W), CONV_W ** -0.5),
        'conv_b': nrm((DEPTH, B_W), 0.02),
        'conv_ln_g': 1.0 + nrm((DEPTH, B_W), 0.05),
        'conv_ln_b': nrm((DEPTH, B_W), 0.02),
        'mix_g': 1.0 + nrm((DEPTH, D_MODEL), 0.05),
        'w_out': nrm((DEPTH, D_MODEL, D_MODEL), d_in),
        'w_up': nrm((DEPTH, D_MODEL, 2 * D_FF), d_in),
        'ffn_conv_w': nrm((DEPTH, FFN_CONV_W, 2 * D_FF), FFN_CONV_W ** -0.5),
        'ffn_conv_b': nrm((DEPTH, 2 * D_FF), 0.02),
        'w_down': nrm((DEPTH, D_FF, D_MODEL), D_FF ** -0.5),
        'final_g': 1.0 + nrm((D_MODEL,), 0.05),
    }


def reference(x_prompt, x_sample, cache_k, cache_v, cache_logf, state_conv, state_ffn_conv, page_table,
              c_prompt, c_sample, norm1_g, norm2_g, w_ada, b_ada, w_in, b_forget, a_ln_g, a_ln_b, w_s, b_s,
              conv_w, conv_b, conv_ln_g, conv_ln_b, mix_g, w_out, w_up, ffn_conv_w, ffn_conv_b, w_down, final_g):
    layer_params = (norm1_g, norm2_g, w_ada, b_ada, w_in, b_forget, a_ln_g, a_ln_b, w_s, b_s,
                    conv_w, conv_b, conv_ln_g, conv_ln_b, mix_g, w_out, w_up, ffn_conv_w, ffn_conv_b, w_down)
    n_seq, n_pages = page_table.shape
    past_len = n_pages * PAGE_SIZE
    xp, xs = x_prompt, x_sample
    pk, pv, pf, pc, pfc = [], [], [], [], []
    sk, sv, sf, sc, sfc, sa = [], [], [], [], [], []
    for l in range(DEPTH):
        lp = [p[l] for p in layer_params]
        conv0 = jnp.zeros((xp.shape[0], CONV_W - 1, B_W), xp.dtype)
        ffn0 = jnp.zeros((xp.shape[0], FFN_CONV_W - 1, 2 * D_FF), xp.dtype)
        xp, (k, v, f, cb, fb, _) = trunk_layer(xp, c_prompt, conv0, ffn0, forget_attention_prompt, *lp)
        pk.append(k); pv.append(v); pf.append(f); pc.append(cb); pfc.append(fb)
        past_k = jnp.take(cache_k[l], page_table, axis=0).reshape(n_seq, past_len, H_C, HEAD_DIM)
        past_v = jnp.take(cache_v[l], page_table, axis=0).reshape(n_seq, past_len, H_C, HEAD_DIM)
        past_f = jnp.take(cache_logf[l], page_table, axis=0).reshape(n_seq, past_len, H_C)
        attend = functools.partial(forget_attention_sample, past_k=past_k, past_v=past_v, past_logf=past_f)
        xs, (k, v, f, cb, fb, av) = trunk_layer(xs, c_sample, state_conv[l], state_ffn_conv[l], attend, *lp)
        sk.append(k); sv.append(v); sf.append(f); sc.append(cb); sfc.append(fb); sa.append(av)
    y_prompt = rmsnorm(xp, final_g)
    y_sample = rmsnorm(xs, final_g)
    return (y_prompt, y_sample, jnp.stack(pk), jnp.stack(pv), jnp.stack(pf), jnp.stack(pc), jnp.stack(pfc),
            jnp.stack(sk), jnp.stack(sv), jnp.stack(sf), jnp.stack(sc), jnp.stack(sfc), jnp.stack(sa))
```

```python
import functools

import jax
import jax.numpy as jnp
from jax import lax
from jax.experimental import pallas as pl
from jax.experimental.pallas import tpu as pltpu

F32 = jnp.float32
BF16 = jnp.bfloat16

HEAD_DIM = 64
CHUNK = 128
PAGE = 128
EPS = 1e-6
SCALE = HEAD_DIM ** -0.5
NEG = -1e30

LANES = 128
SUBLANES = 8
MXU_N = 256
VMEM_LIMIT = 56 * 1024 * 1024

ROW_TILE = 512
CONV_ROWS = 64
HALO = 32
PAGES_PER_STEP = 8


def _dot(a, b):
    return jnp.dot(a, b, preferred_element_type=F32)


def _dot_nt(a, b):
    return lax.dot_general(a, b, (((1,), (1,)), ((), ())), preferred_element_type=F32)


def _rms(x):
    return x * lax.rsqrt(jnp.mean(x * x, axis=-1, keepdims=True) + EPS)


def _layernorm(x, g, b):
    xc = x - jnp.mean(x, axis=-1, keepdims=True)
    return xc * lax.rsqrt(jnp.mean(xc * xc, axis=-1, keepdims=True) + EPS) * g + b


def _gelu(x):
    return 0.5 * x * (1.0 + lax.erf(x * (2.0 ** -0.5)))


def _sigmoid(x):
    return 1.0 / (1.0 + jnp.exp(-x))


def _silu(x):
    return x * _sigmoid(x)


def _log_sigmoid(x):
    return jnp.minimum(x, 0.0) - jnp.log(1.0 + jnp.exp(-jnp.abs(x)))


def _split3(x):
    hi = x.astype(BF16)
    r = x - hi.astype(F32)
    mid = r.astype(BF16)
    lo = (r - mid.astype(F32)).astype(BF16)
    return hi, mid, lo


def _dot3(w01, x):
    hi, mid, lo = _split3(x)
    return _dot(w01, hi) + _dot(w01, mid) + _dot(w01, lo)


def _dot3_r(x, w01):
    hi, mid, lo = _split3(x)
    return _dot(hi, w01) + _dot(mid, w01) + _dot(lo, w01)


def _modulated_norm(x, g, scale, shift):
    return _rms(x) * g * (1.0 + scale) + shift


def _head_selector(n_heads, transpose=False):
    width = n_heads * HEAD_DIM
    shape = (n_heads, width) if transpose else (width, n_heads)
    r = lax.broadcasted_iota(jnp.int32, shape, 0)
    c = lax.broadcasted_iota(jnp.int32, shape, 1)
    feat, head = (c, r) if transpose else (r, c)
    lo = head * HEAD_DIM
    return jnp.where((feat >= lo) & (feat < lo + HEAD_DIM), 1.0, 0.0).astype(BF16)


def _ada_body(c_ref, w_ref, b_ref, o_ref):
    c = c_ref[...]
    o_ref[0] = _dot(_silu(c).astype(BF16), w_ref[0].astype(BF16)) + b_ref[0]


def _ada(c_all, w_ada, b_ada):
    depth, d, n = w_ada.shape
    rows = c_all.shape[0]
    tn = n // 4
    return pl.pallas_call(
        _ada_body,
        grid=(depth, n // tn),
        in_specs=[pl.BlockSpec((rows, d), lambda l, j: (0, 0)),
                  pl.BlockSpec((1, d, tn), lambda l, j: (l, 0, j)),
                  pl.BlockSpec((1, 1, tn), lambda l, j: (l, 0, j))],
        out_specs=pl.BlockSpec((1, rows, tn), lambda l, j: (l, 0, j)),
        out_shape=jax.ShapeDtypeStruct((depth, rows, n), F32),
        compiler_params=pltpu.CompilerParams(dimension_semantics=("arbitrary", "arbitrary"),
                                             vmem_limit_bytes=VMEM_LIMIT),
        name="ada_mod",
    )(c_all, w_ada, b_ada.reshape(depth, 1, n))


def _attention_operands(z_own, cum_hi, cum_mid, cum_lo, h, is_query):
    rows = z_own.shape[0]
    lane = lax.broadcasted_iota(jnp.int32, (rows, LANES), 1)
    parity = h % 2
    own = (lane >= HEAD_DIM * parity) & (lane < HEAD_DIM * (parity + 1))
    base = HEAD_DIM * (1 - parity)
    pieces = [p[:, h:h + 1] for p in (cum_hi, cum_mid, cum_lo)]
    if is_query:
        first, ones_lo = base, base + 3
    else:
        pieces = [-p for p in pieces]
        first, ones_lo = base + 3, base
    extra = jnp.where((lane >= ones_lo) & (lane < ones_lo + 3), 1.0, 0.0)
    for i, p in enumerate(pieces):
        extra = jnp.where(lane == first + i, p, extra)
    return jnp.where(own, z_own, extra).astype(BF16)


def _prompt_in_body(x_ref, mod_ref, n1g_ref, win_ref, wf_ref, bf_ref, alng_ref, alnb_ref, ws_ref, bsf_ref,
                    cw_ref, cb_ref, clng_ref, clnb_ref, mixg_ref,
                    k_ref, v_ref, logf_ref, qh_ref, kh_ref, vb_ref, yab_ref, cst_ref,
                    xp_scr, carry_scr, *, a_w, b_w, c_w, conv_w):
    t = pl.program_id(1)
    tm = x_ref.shape[1]
    n_heads = c_w // HEAD_DIM

    @pl.when(t == 0)
    def _():
        xp_scr[0:HALO, :] = jnp.zeros((HALO, b_w), F32)
        carry_scr[...] = jnp.zeros_like(carry_scr)

    x = x_ref[0]
    h = _modulated_norm(x, n1g_ref[...], mod_ref[0, 1:2, :], mod_ref[0, 0:1, :])
    hb = h.astype(BF16)

    z_uv = _dot(hb, win_ref[:, 0:2 * a_w])
    u = _gelu(z_uv[:, :a_w])
    vln = _layernorm(_gelu(z_uv[:, a_w:]), alng_ref[...], alnb_ref[...]).astype(BF16)
    row = lax.broadcasted_iota(jnp.int32, (CHUNK, CHUNK), 0)
    col = lax.broadcasted_iota(jnp.int32, (CHUNK, CHUNK), 1)
    lane_head = lax.broadcasted_iota(jnp.int32, (CHUNK, a_w), 1) // HEAD_DIM
    w_tril = [jnp.where(col <= row, ws_ref[hh], 0.0).astype(BF16) for hh in range(a_w // HEAD_DIM)]
    sv_chunks = []
    for c in range(tm // CHUNK):
        vc = vln[c * CHUNK:(c + 1) * CHUNK]
        sv = bsf_ref[...]
        for hh, w in enumerate(w_tril):
            sv = sv + jnp.where(lane_head == hh, _dot(w, vc), 0.0)
        sv_chunks.append(sv)
    y_a = _rms(u * jnp.concatenate(sv_chunks, axis=0)) * mixg_ref[:, 0:a_w]
    yab_ref[0, :, 0:a_w] = y_a.astype(BF16)

    z_ag = _dot(hb, win_ref[:, 2 * a_w:2 * a_w + 2 * b_w])
    xp_scr[HALO:HALO + tm, :] = z_ag[:, :b_w] * _sigmoid(z_ag[:, b_w:])
    first_tap = HALO - (conv_w - 1)
    conv_blocks = []
    for r in range(tm // CONV_ROWS):
        acc = jnp.broadcast_to(cb_ref[...], (CONV_ROWS, b_w))
        for j in range(conv_w):
            acc = acc + cw_ref[j:j + 1, :] * xp_scr[pl.ds(r * CONV_ROWS + first_tap + j, CONV_ROWS), :]
        conv_blocks.append(acc)
    conv = jnp.concatenate(conv_blocks, axis=0)
    y_b = _rms(_silu(_layernorm(conv, clng_ref[...], clnb_ref[...]))) * mixg_ref[:, a_w:a_w + b_w]
    yab_ref[0, :, a_w:a_w + b_w] = y_b.astype(BF16)
    cst_ref[0] = xp_scr[pl.ds(HALO + tm - (conv_w - 1), conv_w - 1), :]
    xp_scr[0:HALO, :] = xp_scr[tm:tm + HALO, :]

    o = 2 * a_w + 2 * b_w
    z_q = _dot(hb, win_ref[:, o:o + c_w]) * SCALE
    z_k = _dot(hb, win_ref[:, o + c_w:o + 2 * c_w])
    z_v = _dot(hb, win_ref[:, o + 2 * c_w:o + 3 * c_w])
    k_ref[0] = z_k
    v_ref[0] = z_v
    vb_ref[0] = z_v.astype(BF16)
    logf = _log_sigmoid(_dot(hb, wf_ref[...]) + bf_ref[...])
    logf_ref[0] = logf[:, 0:n_heads]
    rt = lax.broadcasted_iota(jnp.int32, (tm, tm), 0)
    ct = lax.broadcasted_iota(jnp.int32, (tm, tm), 1)
    tri = jnp.where(ct <= rt, 1.0, 0.0).astype(BF16)
    cum = _dot3(tri, logf) + carry_scr[...]
    carry_scr[...] = cum[tm - 1:tm, :]
    cum_hi, cum_mid, cum_lo = (p.astype(F32) for p in _split3(cum))
    for hh in range(n_heads):
        slab = slice((hh // 2) * LANES, (hh // 2 + 1) * LANES)
        qh_ref[0, hh] = _attention_operands(z_q[:, slab], cum_hi, cum_mid, cum_lo, hh, True)
        kh_ref[0, hh] = _attention_operands(z_k[:, slab], cum_hi, cum_mid, cum_lo, hh, False)


def _prompt_in(x, mod, p, tm):
    nb, t, d = x.shape
    a_w, b_w, c_w = p["a_w"], p["b_w"], p["c_w"]
    n_heads = c_w // HEAD_DIM
    conv_w = p["conv_w"].shape[0]
    const = lambda *shape: pl.BlockSpec(shape, lambda b, i: (0,) * len(shape))
    body = functools.partial(_prompt_in_body, a_w=a_w, b_w=b_w, c_w=c_w, conv_w=conv_w)
    return pl.pallas_call(
        body,
        grid=(nb, t // tm),
        in_specs=[pl.BlockSpec((1, tm, d), lambda b, i: (b, i, 0)),
                  pl.BlockSpec((1, 6, d), lambda b, i: (b, 0, 0)),
                  const(1, d), const(*p["w_in"].shape), const(d, LANES), const(1, LANES),
                  const(1, a_w), const(1, a_w), const(*p["w_s"].shape), const(CHUNK, a_w),
                  const(*p["conv_w"].shape), const(1, b_w), const(1, b_w), const(1, b_w), const(1, d)],
        out_specs=[pl.BlockSpec((1, tm, c_w), lambda b, i: (b, i, 0)),
                   pl.BlockSpec((1, tm, c_w), lambda b, i: (b, i, 0)),
                   pl.BlockSpec((1, tm, n_heads), lambda b, i: (b, i, 0)),
                   pl.BlockSpec((1, n_heads, tm, LANES), lambda b, i: (b, 0, i, 0)),
                   pl.BlockSpec((1, n_heads, tm, LANES), lambda b, i: (b, 0, i, 0)),
                   pl.BlockSpec((1, tm, c_w), lambda b, i: (b, i, 0)),
                   pl.BlockSpec((1, tm, a_w + b_w), lambda b, i: (b, i, 0)),
                   pl.BlockSpec((1, conv_w - 1, b_w), lambda b, i: (b, 0, 0))],
        out_shape=[jax.ShapeDtypeStruct((nb, t, c_w), F32),
                   jax.ShapeDtypeStruct((nb, t, c_w), F32),
                   jax.ShapeDtypeStruct((nb, t, n_heads), F32),
                   jax.ShapeDtypeStruct((nb, n_heads, t, LANES), BF16),
                   jax.ShapeDtypeStruct((nb, n_heads, t, LANES), BF16),
                   jax.ShapeDtypeStruct((nb, t, c_w), BF16),
                   jax.ShapeDtypeStruct((nb, t, a_w + b_w), BF16),
                   jax.ShapeDtypeStruct((nb, conv_w - 1, b_w), F32)],
        scratch_shapes=[pltpu.VMEM((HALO + tm, b_w), F32), pltpu.VMEM((1, LANES), F32)],
        compiler_params=pltpu.CompilerParams(dimension_semantics=("arbitrary", "arbitrary"),
                                             vmem_limit_bytes=VMEM_LIMIT),
        name="prompt_in",
    )(x, mod, p["norm1_g"], p["w_in"], p["w_f"], p["b_f"], p["a_ln_g"], p["a_ln_b"], p["w_s"], p["b_s_full"],
      p["conv_w"], p["conv_b"], p["conv_ln_g"], p["conv_ln_b"], p["mix_g"])


def _prompt_attn_body(q_ref, k_ref, v_ref, o_ref):
    i = pl.program_id(2)
    tq = q_ref.shape[2]
    row = lax.broadcasted_iota(jnp.int32, (tq, tq), 0)
    col = lax.broadcasted_iota(jnp.int32, (tq, tq), 1)
    outs = []
    for e in range(2):
        q = q_ref[0, e]

        def tile(j, carry, masked):
            m, l, acc = carry
            start = pl.multiple_of(j * tq, tq)
            s = _dot_nt(q, k_ref[0, e, pl.ds(start, tq), :])
            if masked:
                s = jnp.where(col <= row, s, NEG)
            m_new = jnp.maximum(m, jnp.max(s, axis=-1, keepdims=True))
            alpha = jnp.exp(m - m_new)
            pr = jnp.exp(s - m_new)
            l = alpha * l + jnp.sum(pr, axis=-1, keepdims=True)
            acc = alpha * acc + _dot(pr.astype(BF16), v_ref[0, pl.ds(start, tq), :])
            return m_new, l, acc

        init = (jnp.full((tq, 1), NEG, F32), jnp.zeros((tq, 1), F32), jnp.zeros((tq, LANES), F32))
        carry = lax.fori_loop(0, i, functools.partial(tile, masked=False), init)
        _, l, acc = tile(i, carry, True)
        outs.append(acc / l)
    lane = lax.broadcasted_iota(jnp.int32, (tq, LANES), 1)
    o_ref[0] = jnp.where(lane < HEAD_DIM, outs[0], outs[1])


def _prompt_attn(qh, kh, vb, tq):
    nb, n_heads, t, _ = qh.shape
    return pl.pallas_call(
        _prompt_attn_body,
        grid=(nb, n_heads // 2, t // tq),
        in_specs=[pl.BlockSpec((1, 2, tq, LANES), lambda b, j, i: (b, j, i, 0)),
                  pl.BlockSpec((1, 2, t, LANES), lambda b, j, i: (b, j, 0, 0)),
                  pl.BlockSpec((1, t, LANES), lambda b, j, i: (b, 0, j))],
        out_specs=pl.BlockSpec((1, tq, LANES), lambda b, j, i: (b, i, j)),
        out_shape=jax.ShapeDtypeStruct((nb, t, n_heads * HEAD_DIM), F32),
        compiler_params=pltpu.CompilerParams(dimension_semantics=("arbitrary", "arbitrary", "arbitrary"),
                                             vmem_limit_bytes=VMEM_LIMIT),
        name="prompt_attn",
    )(qh, kh, vb)


def _ffn_chunk(hb, wg, wu, wd, taps_g, taps_u, prev_g, prev_u):
    g = _dot(hb, wg)
    u = _dot(hb, wu)
    cg = taps_g[0] * prev_g[0] + taps_g[1] * prev_g[1] + taps_g[2] * g + taps_g[3]
    cu = taps_u[0] * prev_u[0] + taps_u[1] * prev_u[1] + taps_u[2] * u + taps_u[3]
    return g, u, _dot((_silu(cg) * cu).astype(BF16), wd)


def _prompt_out_body(x_ref, yab_ref, yc_ref, mod_ref, n2g_ref, mixg_ref, wout_ref, wg_ref, wu_ref, wd_ref,
                     taps_ref, fg_ref, o_ref, fst_ref, buf_scr, halo_scr, acc_scr, *, final):
    t = pl.program_id(1)
    tm = x_ref.shape[1]
    ab_w = yab_ref.shape[2]
    n_chunks = wg_ref.shape[0]

    @pl.when(t == 0)
    def _():
        halo_scr[...] = jnp.zeros_like(halo_scr)

    y_c = (_rms(yc_ref[0]) * mixg_ref[:, ab_w:]).astype(BF16)
    attn = _dot(yab_ref[0], wout_ref[0:ab_w, :]) + _dot(y_c, wout_ref[ab_w:, :])
    x1 = x_ref[0] + mod_ref[0, 2:3, :] * attn
    hb = _modulated_norm(x1, n2g_ref[...], mod_ref[0, 4:5, :], mod_ref[0, 3:4, :]).astype(BF16)
    acc_scr[...] = jnp.zeros_like(acc_scr)

    def chunk(c, _):
        prevs = []
        for gu in range(2):
            buf_scr[gu, 0:SUBLANES, :] = halo_scr[gu * n_chunks + c]
        g = _dot(hb, wg_ref[c])
        u = _dot(hb, wu_ref[c])
        buf_scr[0, SUBLANES:SUBLANES + tm, :] = g
        buf_scr[1, SUBLANES:SUBLANES + tm, :] = u
        conv = []
        for gu, cur in ((0, g), (1, u)):
            tp = taps_ref[gu * n_chunks + c]
            conv.append(tp[0:1] * buf_scr[gu, pl.ds(SUBLANES - 2, tm), :]
                        + tp[1:2] * buf_scr[gu, pl.ds(SUBLANES - 1, tm), :]
                        + tp[2:3] * cur + tp[3:4])
            tail = buf_scr[gu, tm:tm + SUBLANES, :]
            halo_scr[gu * n_chunks + c] = tail
            fst_ref[0, gu * n_chunks + c] = tail
        acc_scr[...] += _dot((_silu(conv[0]) * conv[1]).astype(BF16), wd_ref[c])
        return 0

    lax.fori_loop(0, n_chunks, chunk, 0)
    x2 = x1 + mod_ref[0, 5:6, :] * acc_scr[...]
    o_ref[0] = _rms(x2) * fg_ref[...] if final else x2


def _resident(shape):
    return pl.BlockSpec(shape, lambda *_: (0,) * len(shape), pipeline_mode=pl.Buffered(1))


def _prompt_out(x, yab, yc, mod, p, final_g, tm, final):
    nb, t, d = x.shape
    n_chunks = p["w_up_g"].shape[0]
    body = functools.partial(_prompt_out_body, final=final)
    const = lambda *shape: pl.BlockSpec(shape, lambda b, i: (0,) * len(shape))
    return pl.pallas_call(
        body,
        grid=(nb, t // tm),
        in_specs=[pl.BlockSpec((1, tm, d), lambda b, i: (b, i, 0)),
                  pl.BlockSpec((1, tm, yab.shape[2]), lambda b, i: (b, i, 0)),
                  pl.BlockSpec((1, tm, yc.shape[2]), lambda b, i: (b, i, 0)),
                  pl.BlockSpec((1, 6, d), lambda b, i: (b, 0, 0)),
                  const(1, d), const(1, d),
                  _resident(p["w_out"].shape), _resident(p["w_up_g"].shape), _resident(p["w_up_u"].shape),
                  _resident(p["w_down"].shape), _resident(p["ffn_taps"].shape), const(1, d)],
        out_specs=[pl.BlockSpec((1, tm, d), lambda b, i: (b, i, 0)),
                   pl.BlockSpec((1, 2 * n_chunks, SUBLANES, MXU_N), lambda b, i: (b, 0, 0, 0))],
        out_shape=[jax.ShapeDtypeStruct((nb, t, d), F32),
                   jax.ShapeDtypeStruct((nb, 2 * n_chunks, SUBLANES, MXU_N), F32)],
        scratch_shapes=[pltpu.VMEM((2, SUBLANES + tm, MXU_N), F32),
                        pltpu.VMEM((2 * n_chunks, SUBLANES, MXU_N), F32),
                        pltpu.VMEM((tm, d), F32)],
        compiler_params=pltpu.CompilerParams(dimension_semantics=("arbitrary", "arbitrary"),
                                             vmem_limit_bytes=VMEM_LIMIT),
        name="prompt_out",
    )(x, yab, yc, mod, p["norm2_g"], p["mix_g"], p["w_out"], p["w_up_g"], p["w_up_u"], p["w_down"],
      p["ffn_taps"], final_g)


def _sample_in_body(x_ref, mod_ref, n1g_ref, win_ref, wf_ref, bf_ref, alng_ref, alnb_ref, ws0_ref, bs0_ref,
                    cw_ref, cb_ref, clng_ref, clnb_ref, mixg_ref, st_ref,
                    q_ref, k_ref, v_ref, logf_ref, yab_ref, cv_ref, nst_ref, *, a_w, b_w, c_w, conv_w):
    d = x_ref.shape[1]
    n_heads = c_w // HEAD_DIM
    h = _modulated_norm(x_ref[...], n1g_ref[...], mod_ref[:, d:2 * d], mod_ref[:, 0:d])
    hb = h.astype(BF16)
    z = _dot(hb, win_ref[...])
    u = _gelu(z[:, 0:a_w])
    vln = _layernorm(_gelu(z[:, a_w:2 * a_w]), alng_ref[...], alnb_ref[...])
    cv_ref[...] = vln
    y_a = _rms(u * (ws0_ref[...] * vln + bs0_ref[...])) * mixg_ref[:, 0:a_w]
    yab_ref[:, 0:a_w] = y_a.astype(BF16)
    o = 2 * a_w
    glu = z[:, o:o + b_w] * _sigmoid(z[:, o + b_w:o + 2 * b_w])
    hist = (conv_w - 1) * b_w
    acc = cb_ref[...] + cw_ref[conv_w - 1:conv_w, :] * glu
    for j in range(conv_w - 1):
        acc = acc + cw_ref[j:j + 1, :] * st_ref[:, j * b_w:(j + 1) * b_w]
    y_b = _rms(_silu(_layernorm(acc, clng_ref[...], clnb_ref[...]))) * mixg_ref[:, a_w:a_w + b_w]
    yab_ref[:, a_w:a_w + b_w] = y_b.astype(BF16)
    nst_ref[:, 0:hist - b_w] = st_ref[:, b_w:hist]
    nst_ref[:, hist - b_w:hist] = glu
    o += 2 * b_w
    q_ref[...] = z[:, o:o + c_w]
    k_ref[...] = z[:, o + c_w:o + 2 * c_w]
    v_ref[...] = z[:, o + 2 * c_w:o + 3 * c_w]
    logf = _log_sigmoid(_dot(hb, wf_ref[...]) + bf_ref[...])
    logf_ref[...] = logf[:, 0:n_heads]


def _sample_in(x, mod, state, p):
    n, d = x.shape
    a_w, b_w, c_w = p["a_w"], p["b_w"], p["c_w"]
    n_heads = c_w // HEAD_DIM
    conv_w = p["conv_w"].shape[0]
    body = functools.partial(_sample_in_body, a_w=a_w, b_w=b_w, c_w=c_w, conv_w=conv_w)
    outs = [((n, c_w), F32), ((n, c_w), F32), ((n, c_w), F32), ((n, n_heads), F32),
            ((n, a_w + b_w), BF16), ((n, a_w), F32), (state.shape, F32)]
    return pl.pallas_call(
        body,
        out_shape=[jax.ShapeDtypeStruct(s, dt) for s, dt in outs],
        compiler_params=pltpu.CompilerParams(vmem_limit_bytes=VMEM_LIMIT),
        name="sample_in",
    )(x, mod, p["norm1_g"], p["w_in"], p["w_f"], p["b_f"], p["a_ln_g"], p["a_ln_b"], p["w_s0"], p["b_s0"],
      p["conv_w"], p["conv_b"], p["conv_ln_g"], p["conv_ln_b"], p["mix_g"], state)


def _sample_attn_body(pt_ref, q_ref, kn_ref, vn_ref, fn_ref, *refs, n_pages):
    del pt_ref
    k_refs, v_refs, f_refs = refs[:n_pages], refs[n_pages:2 * n_pages], refs[2 * n_pages:3 * n_pages]
    o_ref, m_scr, l_scr, c_scr, acc_scr = refs[3 * n_pages:]
    gi = pl.program_id(1)
    width = q_ref.shape[2]
    n_heads = width // HEAD_DIM
    sel = _head_selector(n_heads)
    sel_t = _head_selector(n_heads, transpose=True)

    @pl.when(gi == 0)
    def _():
        m_scr[...] = jnp.full_like(m_scr, NEG)
        l_scr[...] = jnp.zeros_like(l_scr)
        acc_scr[...] = jnp.zeros_like(acc_scr)
        c_scr[...] = fn_ref[0]

    q = q_ref[0] * SCALE
    r = lax.broadcasted_iota(jnp.int32, (PAGE, PAGE), 0)
    c = lax.broadcasted_iota(jnp.int32, (PAGE, PAGE), 1)
    later = jnp.where(c > r, 1.0, 0.0).astype(BF16)
    carry = c_scr[...]
    scores = [None] * n_pages
    for i in reversed(range(n_pages)):
        logf = f_refs[i][0, 0]
        decay = _dot3(later, logf) + carry
        carry = decay[0:1, :] + logf[0:1, :]
        scores[i] = _dot((k_refs[i][0, 0] * q).astype(BF16), sel) + decay
    c_scr[...] = carry
    m_old = m_scr[...]
    m_new = m_old
    for s in scores:
        m_new = jnp.maximum(m_new, jnp.max(s, axis=0, keepdims=True))
    alpha = jnp.exp(m_old - m_new)
    l = alpha * l_scr[...]
    part = jnp.zeros((SUBLANES, width), F32)
    for i in range(n_pages):
        pr = jnp.exp(scores[i] - m_new)
        l = l + jnp.sum(pr, axis=0, keepdims=True)
        pv = _dot(pr.astype(BF16), sel_t) * v_refs[i][0, 0]
        part = part + jnp.sum(pv.reshape(PAGE // SUBLANES, SUBLANES, width), axis=0)
    m_scr[...] = m_new
    l_scr[...] = l
    acc_scr[...] = _dot3_r(alpha, sel_t) * acc_scr[...] + part

    @pl.when(gi == pl.num_programs(1) - 1)
    def _():
        s_new = _dot(jnp.broadcast_to((kn_ref[0] * q).astype(BF16), (SUBLANES, width)), sel)[0:1, :]
        m_f = jnp.maximum(m_new, s_new)
        a_f = jnp.exp(m_new - m_f)
        p_new = jnp.exp(s_new - m_f)
        l_f = a_f * l + p_new
        past = jnp.sum(acc_scr[...], axis=0, keepdims=True)
        num = _dot3_r(a_f, sel_t) * past + _dot3_r(p_new, sel_t) * vn_ref[0]
        o_ref[0] = num / _dot3_r(l_f, sel_t)


def _sample_attn(q, k_new, v_new, logf_new, cache_k, cache_v, cache_logf, page_table, layer):
    n, width = q.shape
    n_heads = width // HEAD_DIM
    n_seq_pages = page_table.shape[1]
    pps = PAGES_PER_STEP
    n_steps = n_seq_pages // pps
    ck = cache_k.reshape(cache_k.shape[0], cache_k.shape[1], PAGE, width)
    cv = cache_v.reshape(ck.shape)

    def page(i, last):
        return pl.BlockSpec((1, 1, PAGE, last),
                            lambda b, g, pt: (layer, pt[b, (n_steps - 1 - g) * pps + i], 0, 0))

    row = lambda last: pl.BlockSpec((1, 1, last), lambda b, g, pt: (b, 0, 0))
    grid_spec = pltpu.PrefetchScalarGridSpec(
        num_scalar_prefetch=1,
        grid=(n, n_steps),
        in_specs=[row(width), row(width), row(width), row(n_heads)]
                 + [page(i, width) for i in range(pps)] * 2 + [page(i, n_heads) for i in range(pps)],
        out_specs=row(width),
        scratch_shapes=[pltpu.VMEM((1, n_heads), F32), pltpu.VMEM((1, n_heads), F32),
                        pltpu.VMEM((1, n_heads), F32), pltpu.VMEM((SUBLANES, width), F32)],
    )
    r3 = lambda a: a.reshape(n, 1, a.shape[1])
    out = pl.pallas_call(
        functools.partial(_sample_attn_body, n_pages=pps),
        grid_spec=grid_spec,
        out_shape=jax.ShapeDtypeStruct((n, 1, width), F32),
        compiler_params=pltpu.CompilerParams(dimension_semantics=("arbitrary", "arbitrary"),
                                             vmem_limit_bytes=VMEM_LIMIT),
        name="sample_attn",
    )(page_table, r3(q), r3(k_new), r3(v_new), r3(logf_new),
      *([ck] * pps), *([cv] * pps), *([cache_logf] * pps))
    return out.reshape(n, width)


def _sample_out_body(x_ref, yab_ref, yc_ref, mod_ref, n2g_ref, mixg_ref, wout_ref, wg_ref, wu_ref, wd_ref,
                     taps_g_ref, taps_u_ref, s0g_ref, s0u_ref, s1g_ref, s1u_ref, fg_ref,
                     o_ref, ng_ref, nu_ref, pg_ref, pu_ref, x1_scr, hb_scr, acc_scr, *, final):
    c = pl.program_id(0)
    d = x_ref.shape[1]
    ab_w = yab_ref.shape[1]

    @pl.when(c == 0)
    def _():
        y_c = (_rms(yc_ref[...]) * mixg_ref[:, ab_w:]).astype(BF16)
        attn = _dot(yab_ref[...], wout_ref[0:ab_w, :]) + _dot(y_c, wout_ref[ab_w:, :])
        x1 = x_ref[...] + mod_ref[:, 2 * d:3 * d] * attn
        x1_scr[...] = x1
        hb_scr[...] = _modulated_norm(x1, n2g_ref[...], mod_ref[:, 4 * d:5 * d], mod_ref[:, 3 * d:4 * d]).astype(BF16)
        acc_scr[...] = jnp.zeros_like(acc_scr)

    tg, tu = taps_g_ref[0], taps_u_ref[0]
    g, u, down = _ffn_chunk(hb_scr[...], wg_ref[0], wu_ref[0], wd_ref[0],
                            [tg[i:i + 1] for i in range(4)], [tu[i:i + 1] for i in range(4)],
                            (s0g_ref[...], s1g_ref[...]), (s0u_ref[...], s1u_ref[...]))
    ng_ref[...] = g
    nu_ref[...] = u
    pg_ref[...] = s1g_ref[...]
    pu_ref[...] = s1u_ref[...]
    acc_scr[...] += down

    @pl.when(c == pl.num_programs(0) - 1)
    def _():
        x2 = x1_scr[...] + mod_ref[:, 5 * d:6 * d] * acc_scr[...]
        o_ref[...] = _rms(x2) * fg_ref[...] if final else x2


def _sample_out(x, yab, yc, mod, state, p, final_g, final):
    n, d = x.shape
    n_chunks = p["w_up_g"].shape[0]
    d_ff = n_chunks * MXU_N
    const = lambda *shape: pl.BlockSpec(shape, lambda c: (0,) * len(shape))
    st = lambda off: pl.BlockSpec((n, MXU_N), lambda c: (0, off + c))
    col = pl.BlockSpec((n, MXU_N), lambda c: (0, c))
    outs = pl.pallas_call(
        functools.partial(_sample_out_body, final=final),
        grid=(n_chunks,),
        in_specs=[const(n, d), const(*yab.shape), const(*yc.shape), const(*mod.shape), const(1, d), const(1, d),
                  const(*p["w_out"].shape),
                  pl.BlockSpec((1, d, MXU_N), lambda c: (c, 0, 0)),
                  pl.BlockSpec((1, d, MXU_N), lambda c: (c, 0, 0)),
                  pl.BlockSpec((1, MXU_N, d), lambda c: (c, 0, 0)),
                  pl.BlockSpec((1, SUBLANES, MXU_N), lambda c: (c, 0, 0)),
                  pl.BlockSpec((1, SUBLANES, MXU_N), lambda c: (n_chunks + c, 0, 0)),
                  st(0), st(n_chunks), st(2 * n_chunks), st(3 * n_chunks), const(1, d)],
        out_specs=[const(n, d), col, col, col, col],
        out_shape=[jax.ShapeDtypeStruct((n, d), F32)] + [jax.ShapeDtypeStruct((n, d_ff), F32)] * 4,
        scratch_shapes=[pltpu.VMEM((n, d), F32), pltpu.VMEM((n, d), BF16), pltpu.VMEM((n, d), F32)],
        compiler_params=pltpu.CompilerParams(dimension_semantics=("arbitrary",), vmem_limit_bytes=VMEM_LIMIT),
        name="sample_out",
    )(x, yab, yc, mod, p["norm2_g"], p["mix_g"], p["w_out"], p["w_up_g"], p["w_up_u"], p["w_down"],
      p["ffn_taps"], p["ffn_taps"], state, state, state, state, final_g)
    x2, new_g, new_u, prev_g, prev_u = outs
    new_state = jnp.stack([jnp.concatenate([prev_g, prev_u], axis=1), jnp.concatenate([new_g, new_u], axis=1)], axis=1)
    return x2, new_state


def _layer_params(l, d, norm1_g, norm2_g, w_in, b_forget, a_ln_g, a_ln_b, w_s, b_s, conv_w, conv_b,
                  conv_ln_g, conv_ln_b, mix_g, w_out, w_up, ffn_conv_w, ffn_conv_b, w_down):
    a_w, b_w, n_heads = a_ln_g.shape[1], conv_b.shape[1], b_forget.shape[1]
    c_w = n_heads * HEAD_DIM
    main = 2 * a_w + 2 * b_w + 3 * c_w
    d_ff = w_down.shape[1]
    n_chunks = d_ff // MXU_N
    row = lambda v: v.reshape(1, -1)
    pad_lanes = lambda a: jnp.pad(a, ((0, 0), (0, LANES - a.shape[1])))
    chunked = lambda v: v.reshape(2 * n_chunks, MXU_N)
    taps = jnp.stack([chunked(ffn_conv_w[l, 0]), chunked(ffn_conv_w[l, 1]), chunked(ffn_conv_w[l, 2]),
                      chunked(ffn_conv_b[l])], axis=1)
    taps = jnp.pad(taps, ((0, 0), (0, SUBLANES - 4), (0, 0)))
    return dict(
        a_w=a_w, b_w=b_w, c_w=c_w,
        norm1_g=row(norm1_g[l]), norm2_g=row(norm2_g[l]),
        w_in=w_in[l, :, :main].astype(BF16),
        w_f=pad_lanes(w_in[l, :, main:]).astype(BF16), b_f=pad_lanes(row(b_forget[l])),
        a_ln_g=row(a_ln_g[l]), a_ln_b=row(a_ln_b[l]),
        w_s=w_s[l], b_s_full=jnp.repeat(b_s[l].T, HEAD_DIM, axis=1),
        w_s0=row(jnp.repeat(w_s[l, :, 0, 0], HEAD_DIM)), b_s0=row(jnp.repeat(b_s[l, :, 0], HEAD_DIM)),
        conv_w=conv_w[l], conv_b=row(conv_b[l]), conv_ln_g=row(conv_ln_g[l]), conv_ln_b=row(conv_ln_b[l]),
        mix_g=row(mix_g[l]),
        w_out=w_out[l].astype(BF16),
        w_up_g=w_up[l, :, :d_ff].reshape(d, n_chunks, MXU_N).transpose(1, 0, 2).astype(BF16),
        w_up_u=w_up[l, :, d_ff:].reshape(d, n_chunks, MXU_N).transpose(1, 0, 2).astype(BF16),
        w_down=w_down[l].reshape(n_chunks, MXU_N, d).astype(BF16),
        ffn_taps=taps,
    )


def kernel(x_prompt, x_sample, cache_k, cache_v, cache_logf, state_conv, state_ffn_conv, page_table, c_prompt, c_sample, norm1_g, norm2_g, w_ada, b_ada, w_in, b_forget, a_ln_g, a_ln_b, w_s, b_s, conv_w, conv_b, conv_ln_g, conv_ln_b, mix_g, w_out, w_up, ffn_conv_w, ffn_conv_b, w_down, final_g):
    nb, t, d = x_prompt.shape
    ns = x_sample.shape[0]
    depth = w_in.shape[0]
    n_heads = b_forget.shape[1]
    tm = min(ROW_TILE, t)

    c_all = jnp.concatenate([c_prompt, c_sample], axis=0)
    pad_rows = (-c_all.shape[0]) % 16
    mod = _ada(jnp.pad(c_all, ((0, pad_rows), (0, 0))), w_ada, b_ada)
    fg = final_g.reshape(1, d)

    xp, xs = x_prompt, x_sample.reshape(ns, d)
    prompt_out, sample_out = [], []
    for l in range(depth):
        p = _layer_params(l, d, norm1_g, norm2_g, w_in, b_forget, a_ln_g, a_ln_b, w_s, b_s, conv_w, conv_b,
                          conv_ln_g, conv_ln_b, mix_g, w_out, w_up, ffn_conv_w, ffn_conv_b, w_down)
        n_chunks = p["w_up_g"].shape[0]
        final = l == depth - 1
        mod_p = mod[l, :nb].reshape(nb, 6, d)
        mod_s = mod[l, nb:nb + ns]

        k, v, logf, qh, kh, vb, yab, conv_st = _prompt_in(xp, mod_p, p, tm)
        yc = _prompt_attn(qh, kh, vb, tm)
        xp, ffn_tail = _prompt_out(xp, yab, yc, mod_p, p, fg, tm, final)
        ffn_st = ffn_tail[:, :, SUBLANES - 2:, :].transpose(0, 2, 1, 3).reshape(nb, 2, 2 * n_chunks * MXU_N)
        prompt_out.append((k.reshape(nb, t, n_heads, HEAD_DIM), v.reshape(nb, t, n_heads, HEAD_DIM), logf,
                           conv_st, ffn_st))

        st_conv = state_conv[l].reshape(ns, -1)
        q_s, k_s, v_s, logf_s, yab_s, chunk_v, new_conv = _sample_in(xs, mod_s, st_conv, p)
        yc_s = _sample_attn(q_s, k_s, v_s, logf_s, cache_k, cache_v, cache_logf, page_table, l)
        xs, new_ffn = _sample_out(xs, yab_s, yc_s, mod_s, state_ffn_conv[l].reshape(ns, -1), p, fg, final)
        sample_out.append((k_s.reshape(ns, 1, n_heads, HEAD_DIM), v_s.reshape(ns, 1, n_heads, HEAD_DIM),
                           logf_s.reshape(ns, 1, n_heads), new_conv.reshape(state_conv.shape[1:]), new_ffn,
                           chunk_v.reshape(ns, 1, -1)))

    stack = lambda outs, i: jnp.stack([o[i] for o in outs])
    return (xp, xs.reshape(ns, 1, d),
            stack(prompt_out, 0), stack(prompt_out, 1), stack(prompt_out, 2), stack(prompt_out, 3), stack(prompt_out, 4),
            stack(sample_out, 0), stack(sample_out, 1), stack(sample_out, 2), stack(sample_out, 3), stack(sample_out, 4),
            stack(sample_out, 5))
```

```python
import functools

import jax
import jax.numpy as jnp
from jax import lax
from jax.experimental import pallas as pl
from jax.experimental.pallas import tpu as pltpu

F32 = jnp.float32
BF16 = jnp.bfloat16

HEAD_DIM = 64
CHUNK = 128
PAGE = 128
EPS = 1e-6
SCALE = HEAD_DIM ** -0.5
NEG = -1e30
LOG2E = 1.4426950408889634

LANES = 128
SUBLANES = 8
MXU_N = 256
VMEM_LIMIT = 56 * 1024 * 1024

ROW_TILE = 512
CONV_ROWS = 64
ATTN_ROWS = 32
HALO = 32
PAGES_PER_STEP = 16


def _dot(a, b):
    return jnp.dot(a, b, preferred_element_type=F32)


def _dot_nt(a, b):
    return lax.dot_general(a, b, (((1,), (1,)), ((), ())), preferred_element_type=F32)


def _rms(x):
    return x * lax.rsqrt(jnp.mean(x * x, axis=-1, keepdims=True) + EPS)


def _layernorm(x, g, b):
    xc = x - jnp.mean(x, axis=-1, keepdims=True)
    return xc * lax.rsqrt(jnp.mean(xc * xc, axis=-1, keepdims=True) + EPS) * g + b


def _gelu(x):
    return 0.5 * x * (1.0 + lax.erf(x * (2.0 ** -0.5)))


def _sigmoid(x):
    return 1.0 / (1.0 + jnp.exp(-x))


def _silu(x):
    return x * _sigmoid(x)


def _log_sigmoid(x):
    return jnp.minimum(x, 0.0) - jnp.log(1.0 + jnp.exp(-jnp.abs(x)))


def _split3(x):
    hi = x.astype(BF16)
    r = x - hi.astype(F32)
    mid = r.astype(BF16)
    lo = (r - mid.astype(F32)).astype(BF16)
    return hi, mid, lo


def _dot3(w01, x):
    hi, mid, lo = _split3(x)
    return _dot(w01, hi) + _dot(w01, mid) + _dot(w01, lo)


def _dot3_r(x, w01):
    hi, mid, lo = _split3(x)
    return _dot(hi, w01) + _dot(mid, w01) + _dot(lo, w01)


def _modulated_norm(x, g, scale, shift):
    return _rms(x) * g * (1.0 + scale) + shift


def _ada_body(c_ref, w_ref, b_ref, o_ref):
    c = c_ref[...]
    o_ref[0] = _dot(_silu(c).astype(BF16), w_ref[0].astype(BF16)) + b_ref[0]


def _ada(c_all, w_ada, b_ada):
    depth, d, n = w_ada.shape
    rows = c_all.shape[0]
    tn = n // 4
    return pl.pallas_call(
        _ada_body,
        grid=(depth, n // tn),
        in_specs=[pl.BlockSpec((rows, d), lambda l, j: (0, 0)),
                  pl.BlockSpec((1, d, tn), lambda l, j: (l, 0, j)),
                  pl.BlockSpec((1, 1, tn), lambda l, j: (l, 0, j))],
        out_specs=pl.BlockSpec((1, rows, tn), lambda l, j: (l, 0, j)),
        out_shape=jax.ShapeDtypeStruct((depth, rows, n), F32),
        compiler_params=pltpu.CompilerParams(dimension_semantics=("arbitrary", "arbitrary"),
                                             vmem_limit_bytes=VMEM_LIMIT),
        name="ada_mod",
    )(c_all, w_ada, b_ada.reshape(depth, 1, n))


def _own_half(rows, h):
    lane = lax.broadcasted_iota(jnp.int32, (rows, LANES), 1)
    parity = h % 2
    return lane, (lane >= HEAD_DIM * parity) & (lane < HEAD_DIM * (parity + 1))


def _attention_operands(z_own, cum_hi, cum_mid, cum_lo, h, is_query):
    lane, own = _own_half(z_own.shape[0], h)
    base = HEAD_DIM * (1 - h % 2)
    pieces = [p[:, h:h + 1] for p in (cum_hi, cum_mid, cum_lo)]
    if is_query:
        first, ones_lo = base, base + 3
    else:
        pieces = [-p for p in pieces]
        first, ones_lo = base + 3, base
    extra = jnp.where((lane >= ones_lo) & (lane < ones_lo + 3), 1.0, 0.0)
    for i, p in enumerate(pieces):
        extra = jnp.where(lane == first + i, p, extra)
    return jnp.where(own, z_own, extra).astype(BF16)


def _prompt_in_body(x_ref, mod_ref, n1g_ref, win_ref, wf_ref, bf_ref, alng_ref, alnb_ref, ws_ref, bsf_ref,
                    cw_ref, cb_ref, clng_ref, clnb_ref, mixg_ref,
                    k_ref, v_ref, logf_ref, qh_ref, kh_ref, vh_ref, yab_ref, cst_ref,
                    xp_scr, carry_scr, *, a_w, b_w, c_w, conv_w):
    t = pl.program_id(1)
    tm = x_ref.shape[1]
    n_heads = c_w // HEAD_DIM

    @pl.when(t == 0)
    def _():
        xp_scr[0:HALO, :] = jnp.zeros((HALO, b_w), F32)
        carry_scr[...] = jnp.zeros_like(carry_scr)

    x = x_ref[0]
    h = _modulated_norm(x, n1g_ref[...], mod_ref[0, 1:2, :], mod_ref[0, 0:1, :])
    hb = h.astype(BF16)

    z_uv = _dot(hb, win_ref[:, 0:2 * a_w])
    u = _gelu(z_uv[:, :a_w])
    vln = _layernorm(_gelu(z_uv[:, a_w:]), alng_ref[...], alnb_ref[...]).astype(BF16)
    row = lax.broadcasted_iota(jnp.int32, (CHUNK, CHUNK), 0)
    col = lax.broadcasted_iota(jnp.int32, (CHUNK, CHUNK), 1)
    lane_head = lax.broadcasted_iota(jnp.int32, (CHUNK, a_w), 1) // HEAD_DIM
    w_tril = [jnp.where(col <= row, ws_ref[hh], 0.0).astype(BF16) for hh in range(a_w // HEAD_DIM)]
    sv_chunks = []
    for c in range(tm // CHUNK):
        vc = vln[c * CHUNK:(c + 1) * CHUNK]
        sv = bsf_ref[...]
        for hh, w in enumerate(w_tril):
            sv = sv + jnp.where(lane_head == hh, _dot(w, vc), 0.0)
        sv_chunks.append(sv)
    y_a = _rms(u * jnp.concatenate(sv_chunks, axis=0)) * mixg_ref[:, 0:a_w]
    yab_ref[0, :, 0:a_w] = y_a.astype(BF16)

    z_ag = _dot(hb, win_ref[:, 2 * a_w:2 * a_w + 2 * b_w])
    xp_scr[HALO:HALO + tm, :] = z_ag[:, :b_w] * _sigmoid(z_ag[:, b_w:])
    first_tap = HALO - (conv_w - 1)
    conv_blocks = []
    for r in range(tm // CONV_ROWS):
        acc = jnp.broadcast_to(cb_ref[...], (CONV_ROWS, b_w))
        for j in range(conv_w):
            acc = acc + cw_ref[j:j + 1, :] * xp_scr[pl.ds(r * CONV_ROWS + first_tap + j, CONV_ROWS), :]
        conv_blocks.append(acc)
    conv = jnp.concatenate(conv_blocks, axis=0)
    y_b = _rms(_silu(_layernorm(conv, clng_ref[...], clnb_ref[...]))) * mixg_ref[:, a_w:a_w + b_w]
    yab_ref[0, :, a_w:a_w + b_w] = y_b.astype(BF16)
    cst_ref[0] = xp_scr[pl.ds(HALO + tm - (conv_w - 1), conv_w - 1), :]
    xp_scr[0:HALO, :] = xp_scr[tm:tm + HALO, :]

    o = 2 * a_w + 2 * b_w
    z_q = _dot(hb, win_ref[:, o:o + c_w]) * (SCALE * LOG2E)
    z_k = _dot(hb, win_ref[:, o + c_w:o + 2 * c_w])
    z_v = _dot(hb, win_ref[:, o + 2 * c_w:o + 3 * c_w])
    k_ref[0] = z_k
    v_ref[0] = z_v
    logf = _log_sigmoid(_dot(hb, wf_ref[...]) + bf_ref[...])
    logf_ref[0] = logf[:, 0:n_heads]
    rt = lax.broadcasted_iota(jnp.int32, (tm, tm), 0)
    ct = lax.broadcasted_iota(jnp.int32, (tm, tm), 1)
    tri = jnp.where(ct <= rt, 1.0, 0.0).astype(BF16)
    cum = _dot3(tri, logf) + carry_scr[...]
    carry_scr[...] = cum[tm - 1:tm, :]
    cum_hi, cum_mid, cum_lo = (p.astype(F32) for p in _split3(cum * LOG2E))
    for hh in range(n_heads):
        slab = slice((hh // 2) * LANES, (hh // 2 + 1) * LANES)
        qh_ref[0, hh] = _attention_operands(z_q[:, slab], cum_hi, cum_mid, cum_lo, hh, True)
        kh_ref[0, hh] = _attention_operands(z_k[:, slab], cum_hi, cum_mid, cum_lo, hh, False)
        vh_ref[0, hh] = jnp.where(_own_half(tm, hh)[1], z_v[:, slab], 1.0).astype(BF16)


def _prompt_in(x, mod, p, tm):
    nb, t, d = x.shape
    a_w, b_w, c_w = p["a_w"], p["b_w"], p["c_w"]
    n_heads = c_w // HEAD_DIM
    conv_w = p["conv_w"].shape[0]
    const = lambda *shape: pl.BlockSpec(shape, lambda b, i: (0,) * len(shape))
    body = functools.partial(_prompt_in_body, a_w=a_w, b_w=b_w, c_w=c_w, conv_w=conv_w)
    return pl.pallas_call(
        body,
        grid=(nb, t // tm),
        in_specs=[pl.BlockSpec((1, tm, d), lambda b, i: (b, i, 0)),
                  pl.BlockSpec((1, 6, d), lambda b, i: (b, 0, 0)),
                  const(1, d), const(*p["w_in"].shape), const(d, LANES), const(1, LANES),
                  const(1, a_w), const(1, a_w), const(*p["w_s"].shape), const(CHUNK, a_w),
                  const(*p["conv_w"].shape), const(1, b_w), const(1, b_w), const(1, b_w), const(1, d)],
        out_specs=[pl.BlockSpec((1, tm, c_w), lambda b, i: (b, i, 0)),
                   pl.BlockSpec((1, tm, c_w), lambda b, i: (b, i, 0)),
                   pl.BlockSpec((1, tm, n_heads), lambda b, i: (b, i, 0)),
                   pl.BlockSpec((1, n_heads, tm, LANES), lambda b, i: (b, 0, i, 0)),
                   pl.BlockSpec((1, n_heads, tm, LANES), lambda b, i: (b, 0, i, 0)),
                   pl.BlockSpec((1, n_heads, tm, LANES), lambda b, i: (b, 0, i, 0)),
                   pl.BlockSpec((1, tm, a_w + b_w), lambda b, i: (b, i, 0)),
                   pl.BlockSpec((1, conv_w - 1, b_w), lambda b, i: (b, 0, 0))],
        out_shape=[jax.ShapeDtypeStruct((nb, t, c_w), F32),
                   jax.ShapeDtypeStruct((nb, t, c_w), F32),
                   jax.ShapeDtypeStruct((nb, t, n_heads), F32),
                   jax.ShapeDtypeStruct((nb, n_heads, t, LANES), BF16),
                   jax.ShapeDtypeStruct((nb, n_heads, t, LANES), BF16),
                   jax.ShapeDtypeStruct((nb, n_heads, t, LANES), BF16),
                   jax.ShapeDtypeStruct((nb, t, a_w + b_w), BF16),
                   jax.ShapeDtypeStruct((nb, conv_w - 1, b_w), F32)],
        scratch_shapes=[pltpu.VMEM((HALO + tm, b_w), F32), pltpu.VMEM((1, LANES), F32)],
        compiler_params=pltpu.CompilerParams(dimension_semantics=("arbitrary", "arbitrary"),
                                             vmem_limit_bytes=VMEM_LIMIT),
        name="prompt_in",
    )(x, mod, p["norm1_g"], p["w_in"], p["w_f"], p["b_f"], p["a_ln_g"], p["a_ln_b"], p["w_s"], p["b_s_full"],
      p["conv_w"], p["conv_b"], p["conv_ln_g"], p["conv_ln_b"], p["mix_g"])


def _softmax_rows(s_ref, p_ref, m_ref, a_ref, masked):
    tq, tk = s_ref.shape
    for r0 in range(0, tq, ATTN_ROWS):
        rows = slice(r0, r0 + ATTN_ROWS)
        sb = s_ref[rows, :]
        if masked:
            row = r0 + lax.broadcasted_iota(jnp.int32, (ATTN_ROWS, tk), 0)
            col = lax.broadcasted_iota(jnp.int32, (ATTN_ROWS, tk), 1)
            sb = jnp.where(col <= row, sb, NEG)
        m_old = m_ref[rows, :]
        m_new = jnp.maximum(m_old, jnp.max(sb, axis=-1, keepdims=True))
        m_ref[rows, :] = m_new
        a_ref[rows, :] = jnp.exp2(m_old - m_new)
        p_ref[rows, :] = jnp.exp2(sb - m_new).astype(BF16)


def _prompt_attn_body(q_ref, k_ref, v_ref, o_ref, s_scr, p_scr, m_scr, a_scr, acc_scr):
    i = pl.program_id(2)
    tq = q_ref.shape[2]
    m_scr[...] = jnp.full_like(m_scr, NEG)
    acc_scr[...] = jnp.zeros_like(acc_scr)

    def key_tile(j, masked):
        start = pl.multiple_of(j * tq, tq)
        for e in range(2):
            s_scr[e] = _dot_nt(q_ref[0, e], k_ref[0, e, pl.ds(start, tq), :])
        for e in range(2):
            _softmax_rows(s_scr.at[e], p_scr.at[e], m_scr.at[e], a_scr.at[e], masked)
            acc_scr[e] = a_scr[e] * acc_scr[e] + _dot(p_scr[e], v_ref[0, e, pl.ds(start, tq), :])

    def step(j, _):
        key_tile(j, False)
        return 0

    lax.fori_loop(0, i, step, 0)
    key_tile(i, True)
    outs = [acc_scr[e] / pltpu.roll(acc_scr[e], HEAD_DIM, axis=1) for e in range(2)]
    lane = lax.broadcasted_iota(jnp.int32, (tq, LANES), 1)
    o_ref[0] = jnp.where(lane < HEAD_DIM, outs[0], outs[1])


def _prompt_attn(qh, kh, vh, tq):
    nb, n_heads, t, _ = qh.shape
    pair = lambda rows, imap: pl.BlockSpec((1, 2, rows, LANES), imap)
    return pl.pallas_call(
        _prompt_attn_body,
        grid=(nb, n_heads // 2, t // tq),
        in_specs=[pair(tq, lambda b, j, i: (b, j, i, 0)),
                  pair(t, lambda b, j, i: (b, j, 0, 0)),
                  pair(t, lambda b, j, i: (b, j, 0, 0))],
        out_specs=pl.BlockSpec((1, tq, LANES), lambda b, j, i: (b, i, j)),
        out_shape=jax.ShapeDtypeStruct((nb, t, n_heads * HEAD_DIM), F32),
        scratch_shapes=[pltpu.VMEM((2, tq, tq), F32), pltpu.VMEM((2, tq, tq), BF16),
                        pltpu.VMEM((2, tq, 1), F32), pltpu.VMEM((2, tq, 1), F32), pltpu.VMEM((2, tq, LANES), F32)],
        compiler_params=pltpu.CompilerParams(dimension_semantics=("arbitrary", "arbitrary", "arbitrary"),
                                             vmem_limit_bytes=VMEM_LIMIT),
        name="prompt_attn",
    )(qh, kh, vh)


def _ffn_chunk(hb, wg, wu, wd, taps_g, taps_u, prev_g, prev_u):
    g = _dot(hb, wg)
    u = _dot(hb, wu)
    cg = taps_g[0] * prev_g[0] + taps_g[1] * prev_g[1] + taps_g[2] * g + taps_g[3]
    cu = taps_u[0] * prev_u[0] + taps_u[1] * prev_u[1] + taps_u[2] * u + taps_u[3]
    return g, u, _dot((_silu(cg) * cu).astype(BF16), wd)


def _prompt_out_body(x_ref, yab_ref, yc_ref, mod_ref, n2g_ref, mixg_ref, wout_ref, wg_ref, wu_ref, wd_ref,
                     taps_ref, fg_ref, o_ref, fst_ref, buf_scr, halo_scr, acc_scr, *, final):
    t = pl.program_id(1)
    tm = x_ref.shape[1]
    ab_w = yab_ref.shape[2]
    n_chunks = wg_ref.shape[0]

    @pl.when(t == 0)
    def _():
        halo_scr[...] = jnp.zeros_like(halo_scr)

    y_c = (_rms(yc_ref[0]) * mixg_ref[:, ab_w:]).astype(BF16)
    attn = _dot(yab_ref[0], wout_ref[0:ab_w, :]) + _dot(y_c, wout_ref[ab_w:, :])
    x1 = x_ref[0] + mod_ref[0, 2:3, :] * attn
    hb = _modulated_norm(x1, n2g_ref[...], mod_ref[0, 4:5, :], mod_ref[0, 3:4, :]).astype(BF16)
    acc_scr[...] = jnp.zeros_like(acc_scr)

    def chunk(c, _):
        for gu in range(2):
            buf_scr[gu, 0:SUBLANES, :] = halo_scr[gu * n_chunks + c]
        g = _dot(hb, wg_ref[c])
        u = _dot(hb, wu_ref[c])
        buf_scr[0, SUBLANES:SUBLANES + tm, :] = g
        buf_scr[1, SUBLANES:SUBLANES + tm, :] = u
        conv = []
        for gu, cur in ((0, g), (1, u)):
            tp = taps_ref[gu * n_chunks + c]
            conv.append(tp[0:1] * buf_scr[gu, pl.ds(SUBLANES - 2, tm), :]
                        + tp[1:2] * buf_scr[gu, pl.ds(SUBLANES - 1, tm), :]
                        + tp[2:3] * cur + tp[3:4])
            tail = buf_scr[gu, tm:tm + SUBLANES, :]
            halo_scr[gu * n_chunks + c] = tail
            fst_ref[0, gu * n_chunks + c] = tail
        acc_scr[...] += _dot((_silu(conv[0]) * conv[1]).astype(BF16), wd_ref[c])
        return 0

    lax.fori_loop(0, n_chunks, chunk, 0)
    x2 = x1 + mod_ref[0, 5:6, :] * acc_scr[...]
    o_ref[0] = _rms(x2) * fg_ref[...] if final else x2


def _resident(shape):
    return pl.BlockSpec(shape, lambda *_: (0,) * len(shape), pipeline_mode=pl.Buffered(1))


def _prompt_out(x, yab, yc, mod, p, final_g, tm, final):
    nb, t, d = x.shape
    n_chunks = p["w_up_g"].shape[0]
    body = functools.partial(_prompt_out_body, final=final)
    const = lambda *shape: pl.BlockSpec(shape, lambda b, i: (0,) * len(shape))
    return pl.pallas_call(
        body,
        grid=(nb, t // tm),
        in_specs=[pl.BlockSpec((1, tm, d), lambda b, i: (b, i, 0)),
                  pl.BlockSpec((1, tm, yab.shape[2]), lambda b, i: (b, i, 0)),
                  pl.BlockSpec((1, tm, yc.shape[2]), lambda b, i: (b, i, 0)),
                  pl.BlockSpec((1, 6, d), lambda b, i: (b, 0, 0)),
                  const(1, d), const(1, d),
                  _resident(p["w_out"].shape), _resident(p["w_up_g"].shape), _resident(p["w_up_u"].shape),
                  _resident(p["w_down"].shape), _resident(p["ffn_taps"].shape), const(1, d)],
        out_specs=[pl.BlockSpec((1, tm, d), lambda b, i: (b, i, 0)),
                   pl.BlockSpec((1, 2 * n_chunks, SUBLANES, MXU_N), lambda b, i: (b, 0, 0, 0))],
        out_shape=[jax.ShapeDtypeStruct((nb, t, d), F32),
                   jax.ShapeDtypeStruct((nb, 2 * n_chunks, SUBLANES, MXU_N), F32)],
        scratch_shapes=[pltpu.VMEM((2, SUBLANES + tm, MXU_N), F32),
                        pltpu.VMEM((2 * n_chunks, SUBLANES, MXU_N), F32),
                        pltpu.VMEM((tm, d), F32)],
        compiler_params=pltpu.CompilerParams(dimension_semantics=("arbitrary", "arbitrary"),
                                             vmem_limit_bytes=VMEM_LIMIT),
        name="prompt_out",
    )(x, yab, yc, mod, p["norm2_g"], p["mix_g"], p["w_out"], p["w_up_g"], p["w_up_u"], p["w_down"],
      p["ffn_taps"], final_g)


def _sample_in_body(x_ref, mod_ref, n1g_ref, win_ref, wf_ref, bf_ref, alng_ref, alnb_ref, ws0_ref, bs0_ref,
                    cw_ref, cb_ref, clng_ref, clnb_ref, mixg_ref, st_ref,
                    q_ref, k_ref, v_ref, logf_ref, yab_ref, cv_ref, nst_ref, *, a_w, b_w, c_w, conv_w):
    d = x_ref.shape[1]
    n_heads = c_w // HEAD_DIM
    h = _modulated_norm(x_ref[...], n1g_ref[...], mod_ref[:, d:2 * d], mod_ref[:, 0:d])
    hb = h.astype(BF16)
    z = _dot(hb, win_ref[...])
    u = _gelu(z[:, 0:a_w])
    vln = _layernorm(_gelu(z[:, a_w:2 * a_w]), alng_ref[...], alnb_ref[...])
    cv_ref[...] = vln
    y_a = _rms(u * (ws0_ref[...] * vln + bs0_ref[...])) * mixg_ref[:, 0:a_w]
    yab_ref[:, 0:a_w] = y_a.astype(BF16)
    o = 2 * a_w
    glu = z[:, o:o + b_w] * _sigmoid(z[:, o + b_w:o + 2 * b_w])
    hist = (conv_w - 1) * b_w
    acc = cb_ref[...] + cw_ref[conv_w - 1:conv_w, :] * glu
    for j in range(conv_w - 1):
        acc = acc + cw_ref[j:j + 1, :] * st_ref[:, j * b_w:(j + 1) * b_w]
    y_b = _rms(_silu(_layernorm(acc, clng_ref[...], clnb_ref[...]))) * mixg_ref[:, a_w:a_w + b_w]
    yab_ref[:, a_w:a_w + b_w] = y_b.astype(BF16)
    nst_ref[:, 0:hist - b_w] = st_ref[:, b_w:hist]
    nst_ref[:, hist - b_w:hist] = glu
    o += 2 * b_w
    q_ref[...] = z[:, o:o + c_w]
    k_ref[...] = z[:, o + c_w:o + 2 * c_w]
    v_ref[...] = z[:, o + 2 * c_w:o + 3 * c_w]
    logf = _log_sigmoid(_dot(hb, wf_ref[...]) + bf_ref[...])
    logf_ref[...] = logf[:, 0:n_heads]


def _sample_in(x, mod, state, p):
    n, d = x.shape
    a_w, b_w, c_w = p["a_w"], p["b_w"], p["c_w"]
    n_heads = c_w // HEAD_DIM
    conv_w = p["conv_w"].shape[0]
    body = functools.partial(_sample_in_body, a_w=a_w, b_w=b_w, c_w=c_w, conv_w=conv_w)
    outs = [((n, c_w), F32), ((n, c_w), F32), ((n, c_w), F32), ((n, n_heads), F32),
            ((n, a_w + b_w), BF16), ((n, a_w), F32), (state.shape, F32)]
    return pl.pallas_call(
        body,
        out_shape=[jax.ShapeDtypeStruct(s, dt) for s, dt in outs],
        compiler_params=pltpu.CompilerParams(vmem_limit_bytes=VMEM_LIMIT),
        name="sample_in",
    )(x, mod, p["norm1_g"], p["w_in"], p["w_f"], p["b_f"], p["a_ln_g"], p["a_ln_b"], p["w_s0"], p["b_s0"],
      p["conv_w"], p["conv_b"], p["conv_ln_g"], p["conv_ln_b"], p["mix_g"], state)


def _sample_attn_body(pt_ref, q_ref, kn_ref, vn_ref, fn_ref, *refs, n_pages):
    del pt_ref
    k_refs, v_refs, f_refs = refs[:n_pages], refs[n_pages:2 * n_pages], refs[2 * n_pages:3 * n_pages]
    o_ref, qb_scr, m_scr, l_scr, c_scr, s_scr, p_scr, acc_scr = refs[3 * n_pages:]
    gi = pl.program_id(1)
    n_heads = q_ref.shape[1]

    @pl.when(gi == 0)
    def _():
        qb_scr[...] = jnp.broadcast_to(q_ref[0] * SCALE, qb_scr.shape)
        m_scr[...] = jnp.full_like(m_scr, NEG)
        l_scr[...] = jnp.zeros_like(l_scr)
        acc_scr[...] = jnp.zeros_like(acc_scr)
        c_scr[...] = fn_ref[0]

    r = lax.broadcasted_iota(jnp.int32, (PAGE, PAGE), 0)
    c = lax.broadcasted_iota(jnp.int32, (PAGE, PAGE), 1)
    later = jnp.where(r > c, 1.0, 0.0).astype(BF16)
    logf = jnp.concatenate([f_refs[i][0, 0] for i in range(n_pages)], axis=0)
    within = _dot3_r(logf, later)
    page_total = within[:, 0:1] + logf[:, 0:1]
    carry = c_scr[...]
    after = [None] * n_pages
    for i in reversed(range(n_pages)):
        after[i] = carry
        carry = carry + page_total[i * n_heads:(i + 1) * n_heads]
        for h in range(n_heads):
            s_scr[pl.ds(i * n_heads + h, 1), :] = jnp.sum(k_refs[i][0, 0, h] * qb_scr[h], axis=0, keepdims=True)
    c_scr[...] = carry
    scores = s_scr[...] + within + jnp.concatenate(after, axis=0)

    m_old = m_scr[...]
    row_max = jnp.max(scores, axis=1, keepdims=True)
    m_new = m_old
    for i in range(n_pages):
        m_new = jnp.maximum(m_new, row_max[i * n_heads:(i + 1) * n_heads])
    alpha = jnp.exp(m_old - m_new)
    pr = jnp.exp(scores - jnp.concatenate([m_new] * n_pages, axis=0))
    p_scr[...] = pr
    row_sum = jnp.sum(pr, axis=1, keepdims=True)
    l = alpha * l_scr[...]
    for i in range(n_pages):
        l = l + row_sum[i * n_heads:(i + 1) * n_heads]
    m_scr[...] = m_new
    l_scr[...] = l
    for h in range(n_heads):
        acc = alpha[h:h + 1, :] * acc_scr[h]
        for i in range(n_pages):
            acc = acc + p_scr[pl.ds(i * n_heads + h, 1), :] * v_refs[i][0, 0, h]
        acc_scr[h] = acc

    @pl.when(gi == pl.num_programs(1) - 1)
    def _():
        for h in range(n_heads):
            s_new = jnp.sum(q_ref[0, h] * SCALE * kn_ref[0, h], axis=0, keepdims=True)
            m_h = m_new[h:h + 1, :]
            m_f = jnp.maximum(m_h, s_new)
            a_f = jnp.exp(m_h - m_f)
            p_new = jnp.exp(s_new - m_f)
            l_f = a_f * l[h:h + 1, :] + p_new
            past = jnp.sum(acc_scr[h], axis=1, keepdims=True)
            o_ref[0, h] = (a_f * past + p_new * vn_ref[0, h]) / l_f


def _sample_attn(q, k_new, v_new, logf_new, cache_k, cache_v, cache_logf, page_table, layer):
    n, width = q.shape
    n_heads = width // HEAD_DIM
    n_seq_pages = page_table.shape[1]
    pps = min(PAGES_PER_STEP, n_seq_pages)
    n_steps = n_seq_pages // pps
    kt = cache_k.transpose(0, 1, 3, 4, 2)
    vt = cache_v.transpose(0, 1, 3, 4, 2)
    ft = cache_logf.transpose(0, 1, 3, 2)

    def page(i, *tail):
        return pl.BlockSpec((1, 1) + tail,
                            lambda b, g, pt: (layer, pt[b, (n_steps - 1 - g) * pps + i]) + (0,) * len(tail))

    col = pl.BlockSpec((1, n_heads, HEAD_DIM, 1), lambda b, g, pt: (b, 0, 0, 0))
    grid_spec = pltpu.PrefetchScalarGridSpec(
        num_scalar_prefetch=1,
        grid=(n, n_steps),
        in_specs=[col, col, col, pl.BlockSpec((1, n_heads, 1), lambda b, g, pt: (b, 0, 0))]
                 + [page(i, n_heads, HEAD_DIM, PAGE) for i in range(pps)] * 2
                 + [page(i, n_heads, PAGE) for i in range(pps)],
        out_specs=col,
        scratch_shapes=[pltpu.VMEM((n_heads, HEAD_DIM, PAGE), F32),
                        pltpu.VMEM((n_heads, 1), F32), pltpu.VMEM((n_heads, 1), F32), pltpu.VMEM((n_heads, 1), F32),
                        pltpu.VMEM((pps * n_heads, PAGE), F32), pltpu.VMEM((pps * n_heads, PAGE), F32),
                        pltpu.VMEM((n_heads, HEAD_DIM, PAGE), F32)],
    )
    as_col = lambda a: a.reshape(n, n_heads, HEAD_DIM, 1)
    out = pl.pallas_call(
        functools.partial(_sample_attn_body, n_pages=pps),
        grid_spec=grid_spec,
        out_shape=jax.ShapeDtypeStruct((n, n_heads, HEAD_DIM, 1), F32),
        compiler_params=pltpu.CompilerParams(dimension_semantics=("arbitrary", "arbitrary"),
                                             vmem_limit_bytes=VMEM_LIMIT),
        name="sample_attn",
    )(page_table, as_col(q), as_col(k_new), as_col(v_new), logf_new.reshape(n, n_heads, 1),
      *([kt] * pps), *([vt] * pps), *([ft] * pps))
    return out.reshape(n, width)


def _sample_out_body(x_ref, yab_ref, yc_ref, mod_ref, n2g_ref, mixg_ref, wout_ref, wg_ref, wu_ref, wd_ref,
                     taps_g_ref, taps_u_ref, s0g_ref, s0u_ref, s1g_ref, s1u_ref, fg_ref,
                     o_ref, ng_ref, nu_ref, pg_ref, pu_ref, x1_scr, hb_scr, acc_scr, *, final):
    c = pl.program_id(0)
    d = x_ref.shape[1]
    ab_w = yab_ref.shape[1]

    @pl.when(c == 0)
    def _():
        y_c = (_rms(yc_ref[...]) * mixg_ref[:, ab_w:]).astype(BF16)
        attn = _dot(yab_ref[...], wout_ref[0:ab_w, :]) + _dot(y_c, wout_ref[ab_w:, :])
        x1 = x_ref[...] + mod_ref[:, 2 * d:3 * d] * attn
        x1_scr[...] = x1
        hb_scr[...] = _modulated_norm(x1, n2g_ref[...], mod_ref[:, 4 * d:5 * d], mod_ref[:, 3 * d:4 * d]).astype(BF16)
        acc_scr[...] = jnp.zeros_like(acc_scr)

    tg, tu = taps_g_ref[0], taps_u_ref[0]
    g, u, down = _ffn_chunk(hb_scr[...], wg_ref[0], wu_ref[0], wd_ref[0],
                            [tg[i:i + 1] for i in range(4)], [tu[i:i + 1] for i in range(4)],
                            (s0g_ref[...], s1g_ref[...]), (s0u_ref[...], s1u_ref[...]))
    ng_ref[...] = g
    nu_ref[...] = u
    pg_ref[...] = s1g_ref[...]
    pu_ref[...] = s1u_ref[...]
    acc_scr[...] += down

    @pl.when(c == pl.num_programs(0) - 1)
    def _():
        x2 = x1_scr[...] + mod_ref[:, 5 * d:6 * d] * acc_scr[...]
        o_ref[...] = _rms(x2) * fg_ref[...] if final else x2


def _sample_out(x, yab, yc, mod, state, p, final_g, final):
    n, d = x.shape
    n_chunks = p["w_up_g"].shape[0]
    d_ff = n_chunks * MXU_N
    const = lambda *shape: pl.BlockSpec(shape, lambda c: (0,) * len(shape))
    st = lambda off: pl.BlockSpec((n, MXU_N), lambda c: (0, off + c))
    col = pl.BlockSpec((n, MXU_N), lambda c: (0, c))
    outs = pl.pallas_call(
        functools.partial(_sample_out_body, final=final),
        grid=(n_chunks,),
        in_specs=[const(n, d), const(*yab.shape), const(*yc.shape), const(*mod.shape), const(1, d), const(1, d),
                  const(*p["w_out"].shape),
                  pl.BlockSpec((1, d, MXU_N), lambda c: (c, 0, 0)),
                  pl.BlockSpec((1, d, MXU_N), lambda c: (c, 0, 0)),
                  pl.BlockSpec((1, MXU_N, d), lambda c: (c, 0, 0)),
                  pl.BlockSpec((1, SUBLANES, MXU_N), lambda c: (c, 0, 0)),
                  pl.BlockSpec((1, SUBLANES, MXU_N), lambda c: (n_chunks + c, 0, 0)),
                  st(0), st(n_chunks), st(2 * n_chunks), st(3 * n_chunks), const(1, d)],
        out_specs=[const(n, d), col, col, col, col],
        out_shape=[jax.ShapeDtypeStruct((n, d), F32)] + [jax.ShapeDtypeStruct((n, d_ff), F32)] * 4,
        scratch_shapes=[pltpu.VMEM((n, d), F32), pltpu.VMEM((n, d), BF16), pltpu.VMEM((n, d), F32)],
        compiler_params=pltpu.CompilerParams(dimension_semantics=("arbitrary",), vmem_limit_bytes=VMEM_LIMIT),
        name="sample_out",
    )(x, yab, yc, mod, p["norm2_g"], p["mix_g"], p["w_out"], p["w_up_g"], p["w_up_u"], p["w_down"],
      p["ffn_taps"], p["ffn_taps"], state, state, state, state, final_g)
    x2, new_g, new_u, prev_g, prev_u = outs
    new_state = jnp.stack([jnp.concatenate([prev_g, prev_u], axis=1), jnp.concatenate([new_g, new_u], axis=1)], axis=1)
    return x2, new_state


def _layer_params(l, d, norm1_g, norm2_g, w_in, b_forget, a_ln_g, a_ln_b, w_s, b_s, conv_w, conv_b,
                  conv_ln_g, conv_ln_b, mix_g, w_out, w_up, ffn_conv_w, ffn_conv_b, w_down):
    a_w, b_w, n_heads = a_ln_g.shape[1], conv_b.shape[1], b_forget.shape[1]
    c_w = n_heads * HEAD_DIM
    main = 2 * a_w + 2 * b_w + 3 * c_w
    d_ff = w_down.shape[1]
    n_chunks = d_ff // MXU_N
    row = lambda v: v.reshape(1, -1)
    pad_lanes = lambda a: jnp.pad(a, ((0, 0), (0, LANES - a.shape[1])))
    chunked = lambda v: v.reshape(2 * n_chunks, MXU_N)
    taps = jnp.stack([chunked(ffn_conv_w[l, 0]), chunked(ffn_conv_w[l, 1]), chunked(ffn_conv_w[l, 2]),
                      chunked(ffn_conv_b[l])], axis=1)
    taps = jnp.pad(taps, ((0, 0), (0, SUBLANES - 4), (0, 0)))
    return dict(
        a_w=a_w, b_w=b_w, c_w=c_w,
        norm1_g=row(norm1_g[l]), norm2_g=row(norm2_g[l]),
        w_in=w_in[l, :, :main].astype(BF16),
        w_f=pad_lanes(w_in[l, :, main:]).astype(BF16), b_f=pad_lanes(row(b_forget[l])),
        a_ln_g=row(a_ln_g[l]), a_ln_b=row(a_ln_b[l]),
        w_s=w_s[l], b_s_full=jnp.repeat(b_s[l].T, HEAD_DIM, axis=1),
        w_s0=row(jnp.repeat(w_s[l, :, 0, 0], HEAD_DIM)), b_s0=row(jnp.repeat(b_s[l, :, 0], HEAD_DIM)),
        conv_w=conv_w[l], conv_b=row(conv_b[l]), conv_ln_g=row(conv_ln_g[l]), conv_ln_b=row(conv_ln_b[l]),
        mix_g=row(mix_g[l]),
        w_out=w_out[l].astype(BF16),
        w_up_g=w_up[l, :, :d_ff].reshape(d, n_chunks, MXU_N).transpose(1, 0, 2).astype(BF16),
        w_up_u=w_up[l, :, d_ff:].reshape(d, n_chunks, MXU_N).transpose(1, 0, 2).astype(BF16),
        w_down=w_down[l].reshape(n_chunks, MXU_N, d).astype(BF16),
        ffn_taps=taps,
    )


def kernel(x_prompt, x_sample, cache_k, cache_v, cache_logf, state_conv, state_ffn_conv, page_table, c_prompt, c_sample, norm1_g, norm2_g, w_ada, b_ada, w_in, b_forget, a_ln_g, a_ln_b, w_s, b_s, conv_w, conv_b, conv_ln_g, conv_ln_b, mix_g, w_out, w_up, ffn_conv_w, ffn_conv_b, w_down, final_g):
    nb, t, d = x_prompt.shape
    ns = x_sample.shape[0]
    depth = w_in.shape[0]
    n_heads = b_forget.shape[1]
    tm = min(ROW_TILE, t)

    c_all = jnp.concatenate([c_prompt, c_sample], axis=0)
    pad_rows = (-c_all.shape[0]) % 16
    mod = _ada(jnp.pad(c_all, ((0, pad_rows), (0, 0))), w_ada, b_ada)
    fg = final_g.reshape(1, d)

    xp, xs = x_prompt, x_sample.reshape(ns, d)
    prompt_out, sample_out = [], []
    for l in range(depth):
        p = _layer_params(l, d, norm1_g, norm2_g, w_in, b_forget, a_ln_g, a_ln_b, w_s, b_s, conv_w, conv_b,
                          conv_ln_g, conv_ln_b, mix_g, w_out, w_up, ffn_conv_w, ffn_conv_b, w_down)
        n_chunks = p["w_up_g"].shape[0]
        final = l == depth - 1
        mod_p = mod[l, :nb].reshape(nb, 6, d)
        mod_s = mod[l, nb:nb + ns]

        k, v, logf, qh, kh, vh, yab, conv_st = _prompt_in(xp, mod_p, p, tm)
        yc = _prompt_attn(qh, kh, vh, tm)
        xp, ffn_tail = _prompt_out(xp, yab, yc, mod_p, p, fg, tm, final)
        ffn_st = ffn_tail[:, :, SUBLANES - 2:, :].transpose(0, 2, 1, 3).reshape(nb, 2, 2 * n_chunks * MXU_N)
        prompt_out.append((k.reshape(nb, t, n_heads, HEAD_DIM), v.reshape(nb, t, n_heads, HEAD_DIM), logf,
                           conv_st, ffn_st))

        st_conv = state_conv[l].reshape(ns, -1)
        q_s, k_s, v_s, logf_s, yab_s, chunk_v, new_conv = _sample_in(xs, mod_s, st_conv, p)
        yc_s = _sample_attn(q_s, k_s, v_s, logf_s, cache_k, cache_v, cache_logf, page_table, l)
        xs, new_ffn = _sample_out(xs, yab_s, yc_s, mod_s, state_ffn_conv[l].reshape(ns, -1), p, fg, final)
        sample_out.append((k_s.reshape(ns, 1, n_heads, HEAD_DIM), v_s.reshape(ns, 1, n_heads, HEAD_DIM),
                           logf_s.reshape(ns, 1, n_heads), new_conv.reshape(state_conv.shape[1:]), new_ffn,
                           chunk_v.reshape(ns, 1, -1)))

    stack = lambda outs, i: jnp.stack([o[i] for o in outs])
    return (xp, xs.reshape(ns, 1, d),
            stack(prompt_out, 0), stack(prompt_out, 1), stack(prompt_out, 2), stack(prompt_out, 3), stack(prompt_out, 4),
            stack(sample_out, 0), stack(sample_out, 1), stack(sample_out, 2), stack(sample_out, 3), stack(sample_out, 4),
            stack(sample_out, 5))
```

```python
import functools

import jax
import jax.numpy as jnp
from jax import lax
from jax.experimental import pallas as pl
from jax.experimental.pallas import tpu as pltpu

F32 = jnp.float32
BF16 = jnp.bfloat16

HEAD_DIM = 64
CHUNK = 128
PAGE = 128
EPS = 1e-6
SCALE = HEAD_DIM ** -0.5
NEG = -1e30
LOG2E = 1.4426950408889634

LANES = 128
SUBLANES = 8
MXU_N = 256
VMEM_LIMIT = 56 * 1024 * 1024

ROW_TILE = 512
CONV_ROWS = 64
ATTN_ROWS = 32
HALO = 32
PAGES_PER_STEP = 16


def _dot(a, b):
    return jnp.dot(a, b, preferred_element_type=F32)


def _dot_nt(a, b):
    return lax.dot_general(a, b, (((1,), (1,)), ((), ())), preferred_element_type=F32)


def _rms(x):
    return x * lax.rsqrt(jnp.mean(x * x, axis=-1, keepdims=True) + EPS)


def _layernorm(x, g, b):
    xc = x - jnp.mean(x, axis=-1, keepdims=True)
    return xc * lax.rsqrt(jnp.mean(xc * xc, axis=-1, keepdims=True) + EPS) * g + b


def _gelu(x):
    return 0.5 * x * (1.0 + lax.erf(x * (2.0 ** -0.5)))


def _sigmoid(x):
    return 1.0 / (1.0 + jnp.exp(-x))


def _silu(x):
    return x * _sigmoid(x)


def _log_sigmoid(x):
    return jnp.minimum(x, 0.0) - jnp.log(1.0 + jnp.exp(-jnp.abs(x)))


def _split3(x):
    hi = x.astype(BF16)
    r = x - hi.astype(F32)
    mid = r.astype(BF16)
    lo = (r - mid.astype(F32)).astype(BF16)
    return hi, mid, lo


def _dot3(w01, x):
    hi, mid, lo = _split3(x)
    return _dot(w01, hi) + _dot(w01, mid) + _dot(w01, lo)


def _dot3_r(x, w01):
    hi, mid, lo = _split3(x)
    return _dot(hi, w01) + _dot(mid, w01) + _dot(lo, w01)


def _modulated_norm(x, g, scale, shift):
    return _rms(x) * g * (1.0 + scale) + shift


def _ada_body(c_ref, w_ref, b_ref, o_ref):
    c = c_ref[...]
    o_ref[0] = _dot(_silu(c).astype(BF16), w_ref[0].astype(BF16)) + b_ref[0]


def _ada(c_all, w_ada, b_ada):
    depth, d, n = w_ada.shape
    rows = c_all.shape[0]
    tn = n // 4
    return pl.pallas_call(
        _ada_body,
        grid=(depth, n // tn),
        in_specs=[pl.BlockSpec((rows, d), lambda l, j: (0, 0)),
                  pl.BlockSpec((1, d, tn), lambda l, j: (l, 0, j)),
                  pl.BlockSpec((1, 1, tn), lambda l, j: (l, 0, j))],
        out_specs=pl.BlockSpec((1, rows, tn), lambda l, j: (l, 0, j)),
        out_shape=jax.ShapeDtypeStruct((depth, rows, n), F32),
        compiler_params=pltpu.CompilerParams(dimension_semantics=("arbitrary", "arbitrary"),
                                             vmem_limit_bytes=VMEM_LIMIT),
        name="ada_mod",
    )(c_all, w_ada, b_ada.reshape(depth, 1, n))


def _own_half(rows, h):
    lane = lax.broadcasted_iota(jnp.int32, (rows, LANES), 1)
    parity = h % 2
    return lane, (lane >= HEAD_DIM * parity) & (lane < HEAD_DIM * (parity + 1))


def _attention_operands(z_own, cum_hi, cum_mid, cum_lo, h, is_query):
    lane, own = _own_half(z_own.shape[0], h)
    base = HEAD_DIM * (1 - h % 2)
    pieces = [p[:, h:h + 1] for p in (cum_hi, cum_mid, cum_lo)]
    if is_query:
        first, ones_lo = base, base + 3
    else:
        pieces = [-p for p in pieces]
        first, ones_lo = base + 3, base
    extra = jnp.where((lane >= ones_lo) & (lane < ones_lo + 3), 1.0, 0.0)
    for i, p in enumerate(pieces):
        extra = jnp.where(lane == first + i, p, extra)
    return jnp.where(own, z_own, extra).astype(BF16)


def _prompt_in_body(x_ref, mod_ref, n1g_ref, win_ref, wvt_ref, wf_ref, bf_ref, alng_ref, alnb_ref, ws_ref, bsf_ref,
                    cw_ref, cb_ref, clng_ref, clnb_ref, mixg_ref,
                    k_ref, vt_ref, logf_ref, qh_ref, kh_ref, vh_ref, yab_ref, cst_ref,
                    xp_scr, carry_scr, *, a_w, b_w, c_w, conv_w):
    t = pl.program_id(1)
    tm = x_ref.shape[1]
    n_heads = c_w // HEAD_DIM

    @pl.when(t == 0)
    def _():
        xp_scr[0:HALO, :] = jnp.zeros((HALO, b_w), F32)
        carry_scr[...] = jnp.zeros_like(carry_scr)

    x = x_ref[0]
    h = _modulated_norm(x, n1g_ref[...], mod_ref[0, 1:2, :], mod_ref[0, 0:1, :])
    hb = h.astype(BF16)

    z_uv = _dot(hb, win_ref[:, 0:2 * a_w])
    u = _gelu(z_uv[:, :a_w])
    vln = _layernorm(_gelu(z_uv[:, a_w:]), alng_ref[...], alnb_ref[...]).astype(BF16)
    row = lax.broadcasted_iota(jnp.int32, (CHUNK, CHUNK), 0)
    col = lax.broadcasted_iota(jnp.int32, (CHUNK, CHUNK), 1)
    lane_head = lax.broadcasted_iota(jnp.int32, (CHUNK, a_w), 1) // HEAD_DIM
    w_tril = [jnp.where(col <= row, ws_ref[hh], 0.0).astype(BF16) for hh in range(a_w // HEAD_DIM)]
    sv_chunks = []
    for c in range(tm // CHUNK):
        vc = vln[c * CHUNK:(c + 1) * CHUNK]
        sv = bsf_ref[...]
        for hh, w in enumerate(w_tril):
            sv = sv + jnp.where(lane_head == hh, _dot(w, vc), 0.0)
        sv_chunks.append(sv)
    y_a = _rms(u * jnp.concatenate(sv_chunks, axis=0)) * mixg_ref[:, 0:a_w]
    yab_ref[0, :, 0:a_w] = y_a.astype(BF16)

    z_ag = _dot(hb, win_ref[:, 2 * a_w:2 * a_w + 2 * b_w])
    xp_scr[HALO:HALO + tm, :] = z_ag[:, :b_w] * _sigmoid(z_ag[:, b_w:])
    first_tap = HALO - (conv_w - 1)
    conv_blocks = []
    for r in range(tm // CONV_ROWS):
        acc = jnp.broadcast_to(cb_ref[...], (CONV_ROWS, b_w))
        for j in range(conv_w):
            acc = acc + cw_ref[j:j + 1, :] * xp_scr[pl.ds(r * CONV_ROWS + first_tap + j, CONV_ROWS), :]
        conv_blocks.append(acc)
    conv = jnp.concatenate(conv_blocks, axis=0)
    y_b = _rms(_silu(_layernorm(conv, clng_ref[...], clnb_ref[...]))) * mixg_ref[:, a_w:a_w + b_w]
    yab_ref[0, :, a_w:a_w + b_w] = y_b.astype(BF16)
    cst_ref[0] = xp_scr[pl.ds(HALO + tm - (conv_w - 1), conv_w - 1), :]
    xp_scr[0:HALO, :] = xp_scr[tm:tm + HALO, :]

    o = 2 * a_w + 2 * b_w
    z_q = _dot(hb, win_ref[:, o:o + c_w]) * (SCALE * LOG2E)
    z_k = _dot(hb, win_ref[:, o + c_w:o + 2 * c_w])
    z_vt = _dot_nt(wvt_ref[...], hb)
    k_ref[0] = z_k
    vt_ref[0] = z_vt
    logf = _log_sigmoid(_dot(hb, wf_ref[...]) + bf_ref[...])
    logf_ref[0] = logf[:, 0:n_heads]
    rt = lax.broadcasted_iota(jnp.int32, (tm, tm), 0)
    ct = lax.broadcasted_iota(jnp.int32, (tm, tm), 1)
    tri = jnp.where(ct <= rt, 1.0, 0.0).astype(BF16)
    cum = _dot3(tri, logf) + carry_scr[...]
    carry_scr[...] = cum[tm - 1:tm, :]
    cum_hi, cum_mid, cum_lo = (p.astype(F32) for p in _split3(cum * LOG2E))
    for hh in range(n_heads):
        slab = slice((hh // 2) * LANES, (hh // 2 + 1) * LANES)
        qh_ref[0, hh] = _attention_operands(z_q[:, slab], cum_hi, cum_mid, cum_lo, hh, True)
        kh_ref[0, hh] = _attention_operands(z_k[:, slab], cum_hi, cum_mid, cum_lo, hh, False)
        feat = lax.broadcasted_iota(jnp.int32, (LANES, tm), 0) // HEAD_DIM
        vh_ref[0, hh, 0] = jnp.where(feat == hh % 2, z_vt[slab, :], 1.0).astype(BF16)


def _prompt_in(x, mod, p, tm):
    nb, t, d = x.shape
    a_w, b_w, c_w = p["a_w"], p["b_w"], p["c_w"]
    n_heads = c_w // HEAD_DIM
    conv_w = p["conv_w"].shape[0]
    const = lambda *shape: pl.BlockSpec(shape, lambda b, i: (0,) * len(shape))
    body = functools.partial(_prompt_in_body, a_w=a_w, b_w=b_w, c_w=c_w, conv_w=conv_w)
    return pl.pallas_call(
        body,
        grid=(nb, t // tm),
        in_specs=[pl.BlockSpec((1, tm, d), lambda b, i: (b, i, 0)),
                  pl.BlockSpec((1, 6, d), lambda b, i: (b, 0, 0)),
                  const(1, d), const(*p["w_in"].shape), const(*p["w_vt"].shape), const(d, LANES), const(1, LANES),
                  const(1, a_w), const(1, a_w), const(*p["w_s"].shape), const(CHUNK, a_w),
                  const(*p["conv_w"].shape), const(1, b_w), const(1, b_w), const(1, b_w), const(1, d)],
        out_specs=[pl.BlockSpec((1, tm, c_w), lambda b, i: (b, i, 0)),
                   pl.BlockSpec((1, c_w, tm), lambda b, i: (b, 0, i)),
                   pl.BlockSpec((1, tm, n_heads), lambda b, i: (b, i, 0)),
                   pl.BlockSpec((1, n_heads, tm, LANES), lambda b, i: (b, 0, i, 0)),
                   pl.BlockSpec((1, n_heads, tm, LANES), lambda b, i: (b, 0, i, 0)),
                   pl.BlockSpec((1, n_heads, 1, LANES, tm), lambda b, i: (b, 0, i, 0, 0)),
                   pl.BlockSpec((1, tm, a_w + b_w), lambda b, i: (b, i, 0)),
                   pl.BlockSpec((1, conv_w - 1, b_w), lambda b, i: (b, 0, 0))],
        out_shape=[jax.ShapeDtypeStruct((nb, t, c_w), F32),
                   jax.ShapeDtypeStruct((nb, c_w, t), F32),
                   jax.ShapeDtypeStruct((nb, t, n_heads), F32),
                   jax.ShapeDtypeStruct((nb, n_heads, t, LANES), BF16),
                   jax.ShapeDtypeStruct((nb, n_heads, t, LANES), BF16),
                   jax.ShapeDtypeStruct((nb, n_heads, t // tm, LANES, tm), BF16),
                   jax.ShapeDtypeStruct((nb, t, a_w + b_w), BF16),
                   jax.ShapeDtypeStruct((nb, conv_w - 1, b_w), F32)],
        scratch_shapes=[pltpu.VMEM((HALO + tm, b_w), F32), pltpu.VMEM((1, LANES), F32)],
        compiler_params=pltpu.CompilerParams(dimension_semantics=("arbitrary", "arbitrary"),
                                             vmem_limit_bytes=VMEM_LIMIT),
        name="prompt_in",
    )(x, mod, p["norm1_g"], p["w_in"], p["w_vt"], p["w_f"], p["b_f"], p["a_ln_g"], p["a_ln_b"], p["w_s"],
      p["b_s_full"], p["conv_w"], p["conv_b"], p["conv_ln_g"], p["conv_ln_b"], p["mix_g"])


def _softmax_keys_on_rows(s_ref, p_ref, m_ref, a_ref, first_key):
    tk, tq = s_ref.shape

    def chunk(r0, c0):
        sc = s_ref[r0:r0 + ATTN_ROWS, c0:c0 + LANES]
        if first_key is not None:
            key = first_key + r0 + lax.broadcasted_iota(jnp.int32, (ATTN_ROWS, LANES), 0)
            qry = c0 + lax.broadcasted_iota(jnp.int32, (ATTN_ROWS, LANES), 1)
            sc = jnp.where(key <= qry, sc, NEG)
        return sc

    for c0 in range(0, tq, LANES):
        cols = slice(c0, c0 + LANES)
        top = chunk(0, c0)
        for r0 in range(ATTN_ROWS, tk, ATTN_ROWS):
            top = jnp.maximum(top, chunk(r0, c0))
        m_old = m_ref[:, cols]
        m_new = jnp.maximum(m_old, jnp.max(top, axis=0, keepdims=True))
        m_ref[:, cols] = m_new
        a_ref[:, cols] = jnp.exp2(m_old - m_new)
        for r0 in range(0, tk, ATTN_ROWS):
            p_ref[r0:r0 + ATTN_ROWS, cols] = jnp.exp2(chunk(r0, c0) - m_new).astype(BF16)


def _prompt_attn_body(q_ref, k_ref, v_ref, o_ref, s_scr, p_scr, m_scr, a_scr, acc_scr):
    i = pl.program_id(2)
    tq = q_ref.shape[2]
    m_scr[...] = jnp.full_like(m_scr, NEG)
    acc_scr[...] = jnp.zeros_like(acc_scr)

    tk = tq // 2

    def scores(j, half):
        start = pl.multiple_of(j * tq + half * tk, tk)
        for e in range(2):
            s_scr[half, e] = _dot_nt(k_ref[0, e, pl.ds(start, tk), :], q_ref[0, e])

    def consume(j, half, masked):
        for e in range(2):
            _softmax_keys_on_rows(s_scr.at[half, e], p_scr.at[half, e], m_scr.at[e], a_scr.at[e],
                                  half * tk if masked else None)
            acc_scr[e] = a_scr[e] * acc_scr[e] + _dot(v_ref[0, e, j, :, half * tk:(half + 1) * tk], p_scr[half, e])

    def past_block(j, _):
        scores(j, 1)
        consume(j, 0, False)
        scores(j + 1, 0)
        consume(j, 1, False)
        return 0

    scores(0, 0)
    lax.fori_loop(0, i, past_block, 0)
    scores(i, 1)
    consume(i, 0, True)
    consume(i, 1, True)
    halves = []
    for e in range(2):
        own, other = slice(e * HEAD_DIM, (e + 1) * HEAD_DIM), slice((1 - e) * HEAD_DIM, (2 - e) * HEAD_DIM)
        halves.append(acc_scr[e, own, :] / acc_scr[e, other, :])
    o_ref[0] = jnp.concatenate(halves, axis=0).T


def _prompt_attn(qh, kh, vh, tq):
    nb, n_heads, t, _ = qh.shape
    pair = lambda rows, imap: pl.BlockSpec((1, 2, rows, LANES), imap)
    return pl.pallas_call(
        _prompt_attn_body,
        grid=(nb, n_heads // 2, t // tq),
        in_specs=[pair(tq, lambda b, j, i: (b, j, i, 0)),
                  pair(t, lambda b, j, i: (b, j, 0, 0)),
                  pl.BlockSpec((1, 2, t // tq, LANES, tq), lambda b, j, i: (b, j, 0, 0, 0))],
        out_specs=pl.BlockSpec((1, tq, LANES), lambda b, j, i: (b, i, j)),
        out_shape=jax.ShapeDtypeStruct((nb, t, n_heads * HEAD_DIM), F32),
        scratch_shapes=[pltpu.VMEM((2, 2, tq // 2, tq), F32), pltpu.VMEM((2, 2, tq // 2, tq), BF16),
                        pltpu.VMEM((2, 1, tq), F32), pltpu.VMEM((2, 1, tq), F32), pltpu.VMEM((2, LANES, tq), F32)],
        compiler_params=pltpu.CompilerParams(dimension_semantics=("arbitrary", "arbitrary", "arbitrary"),
                                             vmem_limit_bytes=VMEM_LIMIT),
        name="prompt_attn",
    )(qh, kh, vh)


def _ffn_chunk(hb, wg, wu, wd, taps_g, taps_u, prev_g, prev_u):
    g = _dot(hb, wg)
    u = _dot(hb, wu)
    cg = taps_g[0] * prev_g[0] + taps_g[1] * prev_g[1] + taps_g[2] * g + taps_g[3]
    cu = taps_u[0] * prev_u[0] + taps_u[1] * prev_u[1] + taps_u[2] * u + taps_u[3]
    return g, u, _dot((_silu(cg) * cu).astype(BF16), wd)


def _prompt_out_body(x_ref, yab_ref, yc_ref, mod_ref, n2g_ref, mixg_ref, wout_ref, wg_ref, wu_ref, wd_ref,
                     taps_ref, fg_ref, o_ref, fst_ref, buf_scr, halo_scr, act_scr, *, final):
    t = pl.program_id(1)
    tm = x_ref.shape[1]
    ab_w = yab_ref.shape[2]
    n_chunks = wg_ref.shape[0]

    @pl.when(t == 0)
    def _():
        halo_scr[...] = jnp.zeros_like(halo_scr)

    y_c = (_rms(yc_ref[0]) * mixg_ref[:, ab_w:]).astype(BF16)
    attn = _dot(yab_ref[0], wout_ref[0:ab_w, :]) + _dot(y_c, wout_ref[ab_w:, :])
    x1 = x_ref[0] + mod_ref[0, 2:3, :] * attn
    hb = _modulated_norm(x1, n2g_ref[...], mod_ref[0, 4:5, :], mod_ref[0, 3:4, :]).astype(BF16)

    for c in range(n_chunks):
        conv = []
        for gu, w_ref in ((0, wg_ref), (1, wu_ref)):
            slot = gu * n_chunks + c
            buf = buf_scr.at[c % 2, gu]
            cur = _dot(hb, w_ref[c])
            buf[0:SUBLANES, :] = halo_scr[slot]
            buf[SUBLANES:SUBLANES + tm, :] = cur
            tp = taps_ref[slot]
            conv.append(tp[0:1] * buf[pl.ds(SUBLANES - 2, tm), :] + tp[1:2] * buf[pl.ds(SUBLANES - 1, tm), :]
                        + tp[2:3] * cur + tp[3:4])
            tail = buf[tm:tm + SUBLANES, :]
            halo_scr[slot] = tail
            fst_ref[0, slot] = tail
        act_scr[:, c * MXU_N:(c + 1) * MXU_N] = (_silu(conv[0]) * conv[1]).astype(BF16)

    x2 = x1 + mod_ref[0, 5:6, :] * _dot(act_scr[...], wd_ref[...])
    o_ref[0] = _rms(x2) * fg_ref[...] if final else x2


def _resident(shape):
    return pl.BlockSpec(shape, lambda *_: (0,) * len(shape), pipeline_mode=pl.Buffered(1))


def _prompt_out(x, yab, yc, mod, p, final_g, tm, final):
    nb, t, d = x.shape
    n_chunks = p["w_up_g"].shape[0]
    body = functools.partial(_prompt_out_body, final=final)
    const = lambda *shape: pl.BlockSpec(shape, lambda b, i: (0,) * len(shape))
    return pl.pallas_call(
        body,
        grid=(nb, t // tm),
        in_specs=[pl.BlockSpec((1, tm, d), lambda b, i: (b, i, 0)),
                  pl.BlockSpec((1, tm, yab.shape[2]), lambda b, i: (b, i, 0)),
                  pl.BlockSpec((1, tm, yc.shape[2]), lambda b, i: (b, i, 0)),
                  pl.BlockSpec((1, 6, d), lambda b, i: (b, 0, 0)),
                  const(1, d), const(1, d),
                  _resident(p["w_out"].shape), _resident(p["w_up_g"].shape), _resident(p["w_up_u"].shape),
                  _resident(p["w_down_full"].shape), _resident(p["ffn_taps"].shape), const(1, d)],
        out_specs=[pl.BlockSpec((1, tm, d), lambda b, i: (b, i, 0)),
                   pl.BlockSpec((1, 2 * n_chunks, SUBLANES, MXU_N), lambda b, i: (b, 0, 0, 0))],
        out_shape=[jax.ShapeDtypeStruct((nb, t, d), F32),
                   jax.ShapeDtypeStruct((nb, 2 * n_chunks, SUBLANES, MXU_N), F32)],
        scratch_shapes=[pltpu.VMEM((2, 2, SUBLANES + tm, MXU_N), F32),
                        pltpu.VMEM((2 * n_chunks, SUBLANES, MXU_N), F32),
                        pltpu.VMEM((tm, n_chunks * MXU_N), BF16)],
        compiler_params=pltpu.CompilerParams(dimension_semantics=("arbitrary", "arbitrary"),
                                             vmem_limit_bytes=VMEM_LIMIT),
        name="prompt_out",
    )(x, yab, yc, mod, p["norm2_g"], p["mix_g"], p["w_out"], p["w_up_g"], p["w_up_u"], p["w_down_full"],
      p["ffn_taps"], final_g)


def _sample_in_body(x_ref, mod_ref, n1g_ref, win_ref, wf_ref, bf_ref, alng_ref, alnb_ref, ws0_ref, bs0_ref,
                    cw_ref, cb_ref, clng_ref, clnb_ref, mixg_ref, st_ref,
                    q_ref, k_ref, v_ref, logf_ref, yab_ref, cv_ref, nst_ref, *, a_w, b_w, c_w, conv_w):
    d = x_ref.shape[1]
    n_heads = c_w // HEAD_DIM
    h = _modulated_norm(x_ref[...], n1g_ref[...], mod_ref[:, d:2 * d], mod_ref[:, 0:d])
    hb = h.astype(BF16)
    z = _dot(hb, win_ref[...])
    u = _gelu(z[:, 0:a_w])
    vln = _layernorm(_gelu(z[:, a_w:2 * a_w]), alng_ref[...], alnb_ref[...])
    cv_ref[...] = vln
    y_a = _rms(u * (ws0_ref[...] * vln + bs0_ref[...])) * mixg_ref[:, 0:a_w]
    yab_ref[:, 0:a_w] = y_a.astype(BF16)
    o = 2 * a_w
    glu = z[:, o:o + b_w] * _sigmoid(z[:, o + b_w:o + 2 * b_w])
    hist = (conv_w - 1) * b_w
    acc = cb_ref[...] + cw_ref[conv_w - 1:conv_w, :] * glu
    for j in range(conv_w - 1):
        acc = acc + cw_ref[j:j + 1, :] * st_ref[:, j * b_w:(j + 1) * b_w]
    y_b = _rms(_silu(_layernorm(acc, clng_ref[...], clnb_ref[...]))) * mixg_ref[:, a_w:a_w + b_w]
    yab_ref[:, a_w:a_w + b_w] = y_b.astype(BF16)
    nst_ref[:, 0:hist - b_w] = st_ref[:, b_w:hist]
    nst_ref[:, hist - b_w:hist] = glu
    o += 2 * b_w
    q_ref[...] = z[:, o:o + c_w]
    k_ref[...] = z[:, o + c_w:o + 2 * c_w]
    v_ref[...] = z[:, o + 2 * c_w:o + 3 * c_w]
    logf = _log_sigmoid(_dot(hb, wf_ref[...]) + bf_ref[...])
    logf_ref[...] = logf[:, 0:n_heads]


def _sample_in(x, mod, state, p):
    n, d = x.shape
    a_w, b_w, c_w = p["a_w"], p["b_w"], p["c_w"]
    n_heads = c_w // HEAD_DIM
    conv_w = p["conv_w"].shape[0]
    body = functools.partial(_sample_in_body, a_w=a_w, b_w=b_w, c_w=c_w, conv_w=conv_w)
    outs = [((n, c_w), F32), ((n, c_w), F32), ((n, c_w), F32), ((n, n_heads), F32),
            ((n, a_w + b_w), BF16), ((n, a_w), F32), (state.shape, F32)]
    return pl.pallas_call(
        body,
        out_shape=[jax.ShapeDtypeStruct(s, dt) for s, dt in outs],
        compiler_params=pltpu.CompilerParams(vmem_limit_bytes=VMEM_LIMIT),
        name="sample_in",
    )(x, mod, p["norm1_g"], p["w_in"], p["w_f"], p["b_f"], p["a_ln_g"], p["a_ln_b"], p["w_s0"], p["b_s0"],
      p["conv_w"], p["conv_b"], p["conv_ln_g"], p["conv_ln_b"], p["mix_g"], state)


def _sample_attn_body(pt_ref, q_ref, kn_ref, vn_ref, fn_ref, *refs, n_pages):
    del pt_ref
    k_refs, v_refs, f_refs = refs[:n_pages], refs[n_pages:2 * n_pages], refs[2 * n_pages:3 * n_pages]
    o_ref, qb_scr, m_scr, l_scr, c_scr, s_scr, p_scr, acc_scr = refs[3 * n_pages:]
    gi = pl.program_id(1)
    n_heads = q_ref.shape[1]

    @pl.when(gi == 0)
    def _():
        qb_scr[...] = jnp.broadcast_to(q_ref[0] * SCALE, qb_scr.shape)
        m_scr[...] = jnp.full_like(m_scr, NEG)
        l_scr[...] = jnp.zeros_like(l_scr)
        acc_scr[...] = jnp.zeros_like(acc_scr)
        c_scr[...] = fn_ref[0]

    r = lax.broadcasted_iota(jnp.int32, (PAGE, PAGE), 0)
    c = lax.broadcasted_iota(jnp.int32, (PAGE, PAGE), 1)
    later = jnp.where(r > c, 1.0, 0.0).astype(BF16)
    logf = jnp.concatenate([f_refs[i][0, 0] for i in range(n_pages)], axis=0)
    within = _dot3_r(logf, later)
    page_total = within[:, 0:1] + logf[:, 0:1]
    carry = c_scr[...]
    after = [None] * n_pages
    for i in reversed(range(n_pages)):
        after[i] = carry
        carry = carry + page_total[i * n_heads:(i + 1) * n_heads]
        for h in range(n_heads):
            s_scr[pl.ds(i * n_heads + h, 1), :] = jnp.sum(k_refs[i][0, 0, h] * qb_scr[h], axis=0, keepdims=True)
    c_scr[...] = carry
    scores = s_scr[...] + within + jnp.concatenate(after, axis=0)

    m_old = m_scr[...]
    row_max = jnp.max(scores, axis=1, keepdims=True)
    m_new = m_old
    for i in range(n_pages):
        m_new = jnp.maximum(m_new, row_max[i * n_heads:(i + 1) * n_heads])
    alpha = jnp.exp(m_old - m_new)
    pr = jnp.exp(scores - jnp.concatenate([m_new] * n_pages, axis=0))
    p_scr[...] = pr
    row_sum = jnp.sum(pr, axis=1, keepdims=True)
    l = alpha * l_scr[...]
    for i in range(n_pages):
        l = l + row_sum[i * n_heads:(i + 1) * n_heads]
    m_scr[...] = m_new
    l_scr[...] = l
    for h in range(n_heads):
        acc = alpha[h:h + 1, :] * acc_scr[h]
        for i in range(n_pages):
            acc = acc + p_scr[pl.ds(i * n_heads + h, 1), :] * v_refs[i][0, 0, h]
        acc_scr[h] = acc

    @pl.when(gi == pl.num_programs(1) - 1)
    def _():
        for h in range(n_heads):
            s_new = jnp.sum(q_ref[0, h] * SCALE * kn_ref[0, h], axis=0, keepdims=True)
            m_h = m_new[h:h + 1, :]
            m_f = jnp.maximum(m_h, s_new)
            a_f = jnp.exp(m_h - m_f)
            p_new = jnp.exp(s_new - m_f)
            l_f = a_f * l[h:h + 1, :] + p_new
            past = jnp.sum(acc_scr[h], axis=1, keepdims=True)
            o_ref[0, h] = (a_f * past + p_new * vn_ref[0, h]) / l_f


def _sample_attn(q, k_new, v_new, logf_new, cache_k, cache_v, cache_logf, page_table, layer):
    n, width = q.shape
    n_heads = width // HEAD_DIM
    n_seq_pages = page_table.shape[1]
    pps = min(PAGES_PER_STEP, n_seq_pages)
    n_steps = n_seq_pages // pps
    kt = cache_k.transpose(0, 1, 3, 4, 2)
    vt = cache_v.transpose(0, 1, 3, 4, 2)
    ft = cache_logf.transpose(0, 1, 3, 2)

    def page(i, *tail):
        return pl.BlockSpec((1, 1) + tail,
                            lambda b, g, pt: (layer, pt[b, (n_steps - 1 - g) * pps + i]) + (0,) * len(tail))

    col = pl.BlockSpec((1, n_heads, HEAD_DIM, 1), lambda b, g, pt: (b, 0, 0, 0))
    grid_spec = pltpu.PrefetchScalarGridSpec(
        num_scalar_prefetch=1,
        grid=(n, n_steps),
        in_specs=[col, col, col, pl.BlockSpec((1, n_heads, 1), lambda b, g, pt: (b, 0, 0))]
                 + [page(i, n_heads, HEAD_DIM, PAGE) for i in range(pps)] * 2
                 + [page(i, n_heads, PAGE) for i in range(pps)],
        out_specs=col,
        scratch_shapes=[pltpu.VMEM((n_heads, HEAD_DIM, PAGE), F32),
                        pltpu.VMEM((n_heads, 1), F32), pltpu.VMEM((n_heads, 1), F32), pltpu.VMEM((n_heads, 1), F32),
                        pltpu.VMEM((pps * n_heads, PAGE), F32), pltpu.VMEM((pps * n_heads, PAGE), F32),
                        pltpu.VMEM((n_heads, HEAD_DIM, PAGE), F32)],
    )
    as_col = lambda a: a.reshape(n, n_heads, HEAD_DIM, 1)
    out = pl.pallas_call(
        functools.partial(_sample_attn_body, n_pages=pps),
        grid_spec=grid_spec,
        out_shape=jax.ShapeDtypeStruct((n, n_heads, HEAD_DIM, 1), F32),
        compiler_params=pltpu.CompilerParams(dimension_semantics=("arbitrary", "arbitrary"),
                                             vmem_limit_bytes=VMEM_LIMIT),
        name="sample_attn",
    )(page_table, as_col(q), as_col(k_new), as_col(v_new), logf_new.reshape(n, n_heads, 1),
      *([kt] * pps), *([vt] * pps), *([ft] * pps))
    return out.reshape(n, width)


def _sample_out_body(x_ref, yab_ref, yc_ref, mod_ref, n2g_ref, mixg_ref, wout_ref, wg_ref, wu_ref, wd_ref,
                     taps_g_ref, taps_u_ref, s0g_ref, s0u_ref, s1g_ref, s1u_ref, fg_ref,
                     o_ref, ng_ref, nu_ref, pg_ref, pu_ref, x1_scr, hb_scr, acc_scr, *, final):
    c = pl.program_id(0)
    d = x_ref.shape[1]
    ab_w = yab_ref.shape[1]

    @pl.when(c == 0)
    def _():
        y_c = (_rms(yc_ref[...]) * mixg_ref[:, ab_w:]).astype(BF16)
        attn = _dot(yab_ref[...], wout_ref[0:ab_w, :]) + _dot(y_c, wout_ref[ab_w:, :])
        x1 = x_ref[...] + mod_ref[:, 2 * d:3 * d] * attn
        x1_scr[...] = x1
        hb_scr[...] = _modulated_norm(x1, n2g_ref[...], mod_ref[:, 4 * d:5 * d], mod_ref[:, 3 * d:4 * d]).astype(BF16)
        acc_scr[...] = jnp.zeros_like(acc_scr)

    tg, tu = taps_g_ref[0], taps_u_ref[0]
    g, u, down = _ffn_chunk(hb_scr[...], wg_ref[0], wu_ref[0], wd_ref[...],
                            [tg[i:i + 1] for i in range(4)], [tu[i:i + 1] for i in range(4)],
                            (s0g_ref[...], s1g_ref[...]), (s0u_ref[...], s1u_ref[...]))
    ng_ref[...] = g
    nu_ref[...] = u
    pg_ref[...] = s1g_ref[...]
    pu_ref[...] = s1u_ref[...]
    acc_scr[...] += down

    @pl.when(c == pl.num_programs(0) - 1)
    def _():
        x2 = x1_scr[...] + mod_ref[:, 5 * d:6 * d] * acc_scr[...]
        o_ref[...] = _rms(x2) * fg_ref[...] if final else x2


def _sample_out(x, yab, yc, mod, state, p, final_g, final):
    n, d = x.shape
    n_chunks = p["w_up_g"].shape[0]
    d_ff = n_chunks * MXU_N
    const = lambda *shape: pl.BlockSpec(shape, lambda c: (0,) * len(shape))
    st = lambda off: pl.BlockSpec((n, MXU_N), lambda c: (0, off + c))
    col = pl.BlockSpec((n, MXU_N), lambda c: (0, c))
    outs = pl.pallas_call(
        functools.partial(_sample_out_body, final=final),
        grid=(n_chunks,),
        in_specs=[const(n, d), const(*yab.shape), const(*yc.shape), const(*mod.shape), const(1, d), const(1, d),
                  const(*p["w_out"].shape),
                  pl.BlockSpec((1, d, MXU_N), lambda c: (c, 0, 0)),
                  pl.BlockSpec((1, d, MXU_N), lambda c: (c, 0, 0)),
                  pl.BlockSpec((MXU_N, d), lambda c: (c, 0)),
                  pl.BlockSpec((1, SUBLANES, MXU_N), lambda c: (c, 0, 0)),
                  pl.BlockSpec((1, SUBLANES, MXU_N), lambda c: (n_chunks + c, 0, 0)),
                  st(0), st(n_chunks), st(2 * n_chunks), st(3 * n_chunks), const(1, d)],
        out_specs=[const(n, d), col, col, col, col],
        out_shape=[jax.ShapeDtypeStruct((n, d), F32)] + [jax.ShapeDtypeStruct((n, d_ff), F32)] * 4,
        scratch_shapes=[pltpu.VMEM((n, d), F32), pltpu.VMEM((n, d), BF16), pltpu.VMEM((n, d), F32)],
        compiler_params=pltpu.CompilerParams(dimension_semantics=("arbitrary",), vmem_limit_bytes=VMEM_LIMIT),
        name="sample_out",
    )(x, yab, yc, mod, p["norm2_g"], p["mix_g"], p["w_out"], p["w_up_g"], p["w_up_u"], p["w_down_full"],
      p["ffn_taps"], p["ffn_taps"], state, state, state, state, final_g)
    x2, new_g, new_u, prev_g, prev_u = outs
    new_state = jnp.stack([jnp.concatenate([prev_g, prev_u], axis=1), jnp.concatenate([new_g, new_u], axis=1)], axis=1)
    return x2, new_state


def _layer_params(l, d, norm1_g, norm2_g, w_in, b_forget, a_ln_g, a_ln_b, w_s, b_s, conv_w, conv_b,
                  conv_ln_g, conv_ln_b, mix_g, w_out, w_up, ffn_conv_w, ffn_conv_b, w_down):
    a_w, b_w, n_heads = a_ln_g.shape[1], conv_b.shape[1], b_forget.shape[1]
    c_w = n_heads * HEAD_DIM
    main = 2 * a_w + 2 * b_w + 3 * c_w
    d_ff = w_down.shape[1]
    n_chunks = d_ff // MXU_N
    row = lambda v: v.reshape(1, -1)
    pad_lanes = lambda a: jnp.pad(a, ((0, 0), (0, LANES - a.shape[1])))
    chunked = lambda v: v.reshape(2 * n_chunks, MXU_N)
    taps = jnp.stack([chunked(ffn_conv_w[l, 0]), chunked(ffn_conv_w[l, 1]), chunked(ffn_conv_w[l, 2]),
                      chunked(ffn_conv_b[l])], axis=1)
    taps = jnp.pad(taps, ((0, 0), (0, SUBLANES - 4), (0, 0)))
    return dict(
        a_w=a_w, b_w=b_w, c_w=c_w,
        norm1_g=row(norm1_g[l]), norm2_g=row(norm2_g[l]),
        w_in=w_in[l, :, :main].astype(BF16),
        w_vt=w_in[l, :, main - c_w:main].T.astype(BF16),
        w_f=pad_lanes(w_in[l, :, main:]).astype(BF16), b_f=pad_lanes(row(b_forget[l])),
        a_ln_g=row(a_ln_g[l]), a_ln_b=row(a_ln_b[l]),
        w_s=w_s[l], b_s_full=jnp.repeat(b_s[l].T, HEAD_DIM, axis=1),
        w_s0=row(jnp.repeat(w_s[l, :, 0, 0], HEAD_DIM)), b_s0=row(jnp.repeat(b_s[l, :, 0], HEAD_DIM)),
        conv_w=conv_w[l], conv_b=row(conv_b[l]), conv_ln_g=row(conv_ln_g[l]), conv_ln_b=row(conv_ln_b[l]),
        mix_g=row(mix_g[l]),
        w_out=w_out[l].astype(BF16),
        w_up_g=w_up[l, :, :d_ff].reshape(d, n_chunks, MXU_N).transpose(1, 0, 2).astype(BF16),
        w_up_u=w_up[l, :, d_ff:].reshape(d, n_chunks, MXU_N).transpose(1, 0, 2).astype(BF16),
        w_down_full=w_down[l].astype(BF16),
        ffn_taps=taps,
    )


def kernel(x_prompt, x_sample, cache_k, cache_v, cache_logf, state_conv, state_ffn_conv, page_table, c_prompt, c_sample, norm1_g, norm2_g, w_ada, b_ada, w_in, b_forget, a_ln_g, a_ln_b, w_s, b_s, conv_w, conv_b, conv_ln_g, conv_ln_b, mix_g, w_out, w_up, ffn_conv_w, ffn_conv_b, w_down, final_g):
    nb, t, d = x_prompt.shape
    ns = x_sample.shape[0]
    depth = w_in.shape[0]
    n_heads = b_forget.shape[1]
    tm = min(ROW_TILE, t)

    c_all = jnp.concatenate([c_prompt, c_sample], axis=0)
    pad_rows = (-c_all.shape[0]) % 16
    mod = _ada(jnp.pad(c_all, ((0, pad_rows), (0, 0))), w_ada, b_ada)
    fg = final_g.reshape(1, d)

    xp, xs = x_prompt, x_sample.reshape(ns, d)
    prompt_out, sample_out = [], []
    for l in range(depth):
        p = _layer_params(l, d, norm1_g, norm2_g, w_in, b_forget, a_ln_g, a_ln_b, w_s, b_s, conv_w, conv_b,
                          conv_ln_g, conv_ln_b, mix_g, w_out, w_up, ffn_conv_w, ffn_conv_b, w_down)
        n_chunks = p["w_up_g"].shape[0]
        final = l == depth - 1
        mod_p = mod[l, :nb].reshape(nb, 6, d)
        mod_s = mod[l, nb:nb + ns]

        k, vt, logf, qh, kh, vh, yab, conv_st = _prompt_in(xp, mod_p, p, tm)
        yc = _prompt_attn(qh, kh, vh, tm)
        xp, ffn_tail = _prompt_out(xp, yab, yc, mod_p, p, fg, tm, final)
        ffn_st = ffn_tail[:, :, SUBLANES - 2:, :].transpose(0, 2, 1, 3).reshape(nb, 2, 2 * n_chunks * MXU_N)
        v_rows = vt.reshape(nb, n_heads, HEAD_DIM, t).transpose(0, 3, 1, 2)
        prompt_out.append((k.reshape(nb, t, n_heads, HEAD_DIM), v_rows, logf, conv_st, ffn_st))

        st_conv = state_conv[l].reshape(ns, -1)
        q_s, k_s, v_s, logf_s, yab_s, chunk_v, new_conv = _sample_in(xs, mod_s, st_conv, p)
        yc_s = _sample_attn(q_s, k_s, v_s, logf_s, cache_k, cache_v, cache_logf, page_table, l)
        xs, new_ffn = _sample_out(xs, yab_s, yc_s, mod_s, state_ffn_conv[l].reshape(ns, -1), p, fg, final)
        sample_out.append((k_s.reshape(ns, 1, n_heads, HEAD_DIM), v_s.reshape(ns, 1, n_heads, HEAD_DIM),
                           logf_s.reshape(ns, 1, n_heads), new_conv.reshape(state_conv.shape[1:]), new_ffn,
                           chunk_v.reshape(ns, 1, -1)))

    stack = lambda outs, i: jnp.stack([o[i] for o in outs])
    return (xp, xs.reshape(ns, 1, d),
            stack(prompt_out, 0), stack(prompt_out, 1), stack(prompt_out, 2), stack(prompt_out, 3), stack(prompt_out, 4),
            stack(sample_out, 0), stack(sample_out, 1), stack(sample_out, 2), stack(sample_out, 3), stack(sample_out, 4),
            stack(sample_out, 5))
```

```python
import functools

import numpy as np

import jax
import jax.numpy as jnp
from jax import lax
from jax.experimental import pallas as pl
from jax.experimental.pallas import tpu as pltpu

F32 = jnp.float32
BF16 = jnp.bfloat16

HEAD_DIM = 64
CHUNK = 128
PAGE = 128
EPS = 1e-6
SCALE = HEAD_DIM ** -0.5
NEG = -1e30
LOG2E = 1.4426950408889634

LANES = 128
SUBLANES = 8
MXU_N = 256
VMEM_LIMIT = 56 * 1024 * 1024

ROW_TILE = 512
CONV_ROWS = 64
ATTN_ROWS = 32
HALO = 32
PAGES_PER_STEP = 16


def _dot(a, b):
    return jnp.dot(a, b, preferred_element_type=F32)


def _dot_nt(a, b):
    return lax.dot_general(a, b, (((1,), (1,)), ((), ())), preferred_element_type=F32)


def _rms(x):
    return x * lax.rsqrt(jnp.mean(x * x, axis=-1, keepdims=True) + EPS)


def _layernorm(x, g, b):
    xc = x - jnp.mean(x, axis=-1, keepdims=True)
    return xc * lax.rsqrt(jnp.mean(xc * xc, axis=-1, keepdims=True) + EPS) * g + b


def _gelu(x):
    return 0.5 * x * (1.0 + lax.erf(x * (2.0 ** -0.5)))


def _sigmoid(x):
    return 1.0 / (1.0 + jnp.exp(-x))


def _silu(x):
    return x * _sigmoid(x)


def _log_sigmoid(x):
    return jnp.minimum(x, 0.0) - jnp.log(1.0 + jnp.exp(-jnp.abs(x)))


def _split3(x):
    hi = x.astype(BF16)
    r = x - hi.astype(F32)
    mid = r.astype(BF16)
    lo = (r - mid.astype(F32)).astype(BF16)
    return hi, mid, lo


def _dot3(w01, x):
    hi, mid, lo = _split3(x)
    return _dot(w01, hi) + _dot(w01, mid) + _dot(w01, lo)


def _dot3_r(x, w01):
    hi, mid, lo = _split3(x)
    return _dot(hi, w01) + _dot(mid, w01) + _dot(lo, w01)


def _diagonal(vec, n):
    r = lax.broadcasted_iota(jnp.int32, (n, n), 0)
    c = lax.broadcasted_iota(jnp.int32, (n, n), 1)
    return jnp.where(r == c, jnp.broadcast_to(vec, (n, n)), 0.0)


def _row_to_col(row):
    return jnp.sum(_diagonal(row, row.shape[1]), axis=1, keepdims=True)


def _col_to_row(col):
    return jnp.sum(_diagonal(col, col.shape[0]), axis=0, keepdims=True)


def _modulated_norm(x, g, scale, shift):
    return _rms(x) * g * (1.0 + scale) + shift


def _ada_body(c_ref, w_ref, b_ref, o_ref):
    c = c_ref[...]
    o_ref[0] = _dot(_silu(c).astype(BF16), w_ref[0].astype(BF16)) + b_ref[0]


def _ada(c_all, w_ada, b_ada):
    depth, d, n = w_ada.shape
    rows = c_all.shape[0]
    tn = n // 4
    return pl.pallas_call(
        _ada_body,
        grid=(depth, n // tn),
        in_specs=[pl.BlockSpec((rows, d), lambda l, j: (0, 0)),
                  pl.BlockSpec((1, d, tn), lambda l, j: (l, 0, j)),
                  pl.BlockSpec((1, 1, tn), lambda l, j: (l, 0, j))],
        out_specs=pl.BlockSpec((1, rows, tn), lambda l, j: (l, 0, j)),
        out_shape=jax.ShapeDtypeStruct((depth, rows, n), F32),
        compiler_params=pltpu.CompilerParams(dimension_semantics=("arbitrary", "arbitrary"),
                                             vmem_limit_bytes=VMEM_LIMIT),
        name="ada_mod",
    )(c_all, w_ada, b_ada.reshape(depth, 1, n))


def _own_half(rows, h):
    lane = lax.broadcasted_iota(jnp.int32, (rows, LANES), 1)
    parity = h % 2
    return lane, (lane >= HEAD_DIM * parity) & (lane < HEAD_DIM * (parity + 1))


def _decay_placement(n_heads):
    pq = np.zeros((LANES, n_heads * LANES), np.float32)
    pk = np.zeros((LANES, n_heads * LANES), np.float32)
    one_lane = 3 * n_heads
    for h in range(n_heads):
        base = h * LANES + HEAD_DIM * (1 - h % 2)
        for i in range(3):
            pq[i * n_heads + h, base + i] = 1.0
            pq[one_lane, base + 3 + i] = 1.0
            pk[one_lane, base + i] = 1.0
            pk[i * n_heads + h, base + 3 + i] = -1.0
    return jnp.asarray(pq, BF16), jnp.asarray(pk, BF16)


def _prompt_in_body(x_ref, mod_ref, n1g_ref, win_ref, wvt_ref, wf_ref, bf_ref, alng_ref, alnb_ref, ws_ref, bsf_ref,
                    cw_ref, cb_ref, clng_ref, clnb_ref, mixg_ref, tri_ref, pq_ref, pk_ref,
                    k_ref, vt_ref, logf_ref, qh_ref, kh_ref, vh_ref, yab_ref, cst_ref,
                    xp_scr, xs_scr, carry_scr, *, a_w, b_w, c_w, conv_w):
    t = pl.program_id(1)
    tm = x_ref.shape[1]
    n_heads = c_w // HEAD_DIM

    @pl.when(t == 0)
    def _():
        xp_scr[0:HALO, :] = jnp.zeros((HALO, b_w), F32)
        carry_scr[...] = jnp.zeros_like(carry_scr)

    x = x_ref[0]
    h = _modulated_norm(x, n1g_ref[...], mod_ref[0, 1:2, :], mod_ref[0, 0:1, :])
    hb = h.astype(BF16)

    z_uv = _dot(hb, win_ref[:, 0:2 * a_w])
    u = _gelu(z_uv[:, :a_w])
    vln = _layernorm(_gelu(z_uv[:, a_w:]), alng_ref[...], alnb_ref[...]).astype(BF16)
    row = lax.broadcasted_iota(jnp.int32, (CHUNK, CHUNK), 0)
    col = lax.broadcasted_iota(jnp.int32, (CHUNK, CHUNK), 1)
    lane_head = lax.broadcasted_iota(jnp.int32, (CHUNK, a_w), 1) // HEAD_DIM
    w_tril = [jnp.where(col <= row, ws_ref[hh], 0.0).astype(BF16) for hh in range(a_w // HEAD_DIM)]
    sv_chunks = []
    for c in range(tm // CHUNK):
        vc = vln[c * CHUNK:(c + 1) * CHUNK]
        sv = bsf_ref[...]
        for hh, w in enumerate(w_tril):
            sv = sv + jnp.where(lane_head == hh, _dot(w, vc), 0.0)
        sv_chunks.append(sv)
    y_a = _rms(u * jnp.concatenate(sv_chunks, axis=0)) * mixg_ref[:, 0:a_w]
    yab_ref[0, :, 0:a_w] = y_a.astype(BF16)

    z_ag = _dot(hb, win_ref[:, 2 * a_w:2 * a_w + 2 * b_w])
    xp_scr[HALO:HALO + tm, :] = z_ag[:, :b_w] * _sigmoid(z_ag[:, b_w:])
    shifted_rows = xs_scr.shape[1]
    for r in range(1, SUBLANES):
        xs_scr[r - 1] = xp_scr[pl.ds(r, shifted_rows), :]
    first_tap = HALO - (conv_w - 1)
    conv_blocks = []
    for rb in range(tm // CONV_ROWS):
        acc = jnp.broadcast_to(cb_ref[...], (CONV_ROWS, b_w))
        for j in range(conv_w):
            whole, r = divmod(first_tap + j, SUBLANES)
            src = xp_scr if r == 0 else xs_scr.at[r - 1]
            acc = acc + cw_ref[j:j + 1, :] * src[pl.ds(rb * CONV_ROWS + whole * SUBLANES, CONV_ROWS), :]
        conv_blocks.append(acc)
    conv = jnp.concatenate(conv_blocks, axis=0)
    y_b = _rms(_silu(_layernorm(conv, clng_ref[...], clnb_ref[...]))) * mixg_ref[:, a_w:a_w + b_w]
    yab_ref[0, :, a_w:a_w + b_w] = y_b.astype(BF16)
    cst_ref[0] = xp_scr[pl.ds(HALO + tm - (conv_w - 1), conv_w - 1), :]
    xp_scr[0:HALO, :] = xp_scr[tm:tm + HALO, :]

    o = 2 * a_w + 2 * b_w
    z_q = _dot(hb, win_ref[:, o:o + c_w]) * (SCALE * LOG2E)
    z_k = _dot(hb, win_ref[:, o + c_w:o + 2 * c_w])
    z_vt = _dot_nt(wvt_ref[...], hb)
    k_ref[0] = z_k
    vt_ref[0] = z_vt
    logf = _log_sigmoid(_dot(hb, wf_ref[...]) + bf_ref[...])
    logf_ref[0] = logf[:, 0:n_heads]
    cum = _dot3(tri_ref[...], logf) + carry_scr[...]
    carry_scr[...] = cum[tm - 1:tm, :]
    hi, mid, lo = _split3(cum * LOG2E)
    lane = lax.broadcasted_iota(jnp.int32, (tm, LANES), 1)
    one = jnp.where(lane == 3 * n_heads, 1.0, 0.0).astype(BF16)
    packed = jnp.where(lane < n_heads, hi, jnp.where(lane < 2 * n_heads, mid, jnp.where(lane < 3 * n_heads, lo, one)))
    decay_q = _dot(packed, pq_ref[...])
    decay_k = _dot(packed, pk_ref[...])
    for hh in range(n_heads):
        slab = slice((hh // 2) * LANES, (hh // 2 + 1) * LANES)
        mine = slice(hh * LANES, (hh + 1) * LANES)
        own = _own_half(tm, hh)[1]
        qh_ref[0, hh] = jnp.where(own, z_q[:, slab], decay_q[:, mine]).astype(BF16)
        kh_ref[0, hh] = jnp.where(own, z_k[:, slab], decay_k[:, mine]).astype(BF16)
        feat = lax.broadcasted_iota(jnp.int32, (LANES, tm), 0) // HEAD_DIM
        vh_ref[0, hh, 0] = jnp.where(feat == hh % 2, z_vt[slab, :], 1.0).astype(BF16)


def _prompt_in(x, mod, p, tm):
    nb, t, d = x.shape
    a_w, b_w, c_w = p["a_w"], p["b_w"], p["c_w"]
    n_heads = c_w // HEAD_DIM
    conv_w = p["conv_w"].shape[0]
    const = lambda *shape: pl.BlockSpec(shape, lambda b, i: (0,) * len(shape))
    body = functools.partial(_prompt_in_body, a_w=a_w, b_w=b_w, c_w=c_w, conv_w=conv_w)
    return pl.pallas_call(
        body,
        grid=(nb, t // tm),
        in_specs=[pl.BlockSpec((1, tm, d), lambda b, i: (b, i, 0)),
                  pl.BlockSpec((1, 6, d), lambda b, i: (b, 0, 0)),
                  const(1, d), const(*p["w_in"].shape), const(*p["w_vt"].shape), const(d, LANES), const(1, LANES),
                  const(1, a_w), const(1, a_w), const(*p["w_s"].shape), const(CHUNK, a_w),
                  const(*p["conv_w"].shape), const(1, b_w), const(1, b_w), const(1, b_w), const(1, d),
                  const(tm, tm), const(*p["decay_q"].shape), const(*p["decay_k"].shape)],
        out_specs=[pl.BlockSpec((1, tm, c_w), lambda b, i: (b, i, 0)),
                   pl.BlockSpec((1, c_w, tm), lambda b, i: (b, 0, i)),
                   pl.BlockSpec((1, tm, n_heads), lambda b, i: (b, i, 0)),
                   pl.BlockSpec((1, n_heads, tm, LANES), lambda b, i: (b, 0, i, 0)),
                   pl.BlockSpec((1, n_heads, tm, LANES), lambda b, i: (b, 0, i, 0)),
                   pl.BlockSpec((1, n_heads, 1, LANES, tm), lambda b, i: (b, 0, i, 0, 0)),
                   pl.BlockSpec((1, tm, a_w + b_w), lambda b, i: (b, i, 0)),
                   pl.BlockSpec((1, conv_w - 1, b_w), lambda b, i: (b, 0, 0))],
        out_shape=[jax.ShapeDtypeStruct((nb, t, c_w), F32),
                   jax.ShapeDtypeStruct((nb, c_w, t), F32),
                   jax.ShapeDtypeStruct((nb, t, n_heads), F32),
                   jax.ShapeDtypeStruct((nb, n_heads, t, LANES), BF16),
                   jax.ShapeDtypeStruct((nb, n_heads, t, LANES), BF16),
                   jax.ShapeDtypeStruct((nb, n_heads, t // tm, LANES, tm), BF16),
                   jax.ShapeDtypeStruct((nb, t, a_w + b_w), BF16),
                   jax.ShapeDtypeStruct((nb, conv_w - 1, b_w), F32)],
        scratch_shapes=[pltpu.VMEM((HALO + tm, b_w), F32),
                        pltpu.VMEM((SUBLANES - 1, HALO + tm - SUBLANES, b_w), F32),
                        pltpu.VMEM((1, LANES), F32)],
        compiler_params=pltpu.CompilerParams(dimension_semantics=("arbitrary", "arbitrary"),
                                             vmem_limit_bytes=VMEM_LIMIT),
        name="prompt_in",
    )(x, mod, p["norm1_g"], p["w_in"], p["w_vt"], p["w_f"], p["b_f"], p["a_ln_g"], p["a_ln_b"], p["w_s"],
      p["b_s_full"], p["conv_w"], p["conv_b"], p["conv_ln_g"], p["conv_ln_b"], p["mix_g"],
      jnp.tril(jnp.ones((tm, tm), BF16)), p["decay_q"], p["decay_k"])


def _softmax_keys_on_rows(s_ref, p_ref, m_ref, a_ref, first_key):
    tk, tq = s_ref.shape

    def visible(r0, c0):
        if first_key is None or first_key + r0 + ATTN_ROWS - 1 <= c0:
            return "all"
        return "none" if first_key + r0 > c0 + LANES - 1 else "some"

    def chunk(r0, c0):
        sc = s_ref[r0:r0 + ATTN_ROWS, c0:c0 + LANES]
        if visible(r0, c0) == "some":
            key = first_key + r0 + lax.broadcasted_iota(jnp.int32, (ATTN_ROWS, LANES), 0)
            qry = c0 + lax.broadcasted_iota(jnp.int32, (ATTN_ROWS, LANES), 1)
            sc = jnp.where(key <= qry, sc, NEG)
        return sc

    for c0 in range(0, tq, LANES):
        cols = slice(c0, c0 + LANES)
        seen = [r0 for r0 in range(0, tk, ATTN_ROWS) if visible(r0, c0) != "none"]
        if not seen:
            a_ref[:, cols] = jnp.ones((1, LANES), F32)
            p_ref[:, cols] = jnp.zeros((tk, LANES), BF16)
            continue
        top = chunk(seen[0], c0)
        for r0 in seen[1:]:
            top = jnp.maximum(top, chunk(r0, c0))
        m_old = m_ref[:, cols]
        m_new = jnp.maximum(m_old, jnp.max(top, axis=0, keepdims=True))
        m_ref[:, cols] = m_new
        a_ref[:, cols] = jnp.exp2(m_old - m_new)
        for r0 in range(0, tk, ATTN_ROWS):
            if r0 in seen:
                p_ref[r0:r0 + ATTN_ROWS, cols] = jnp.exp2(chunk(r0, c0) - m_new).astype(BF16)
            else:
                p_ref[r0:r0 + ATTN_ROWS, cols] = jnp.zeros((ATTN_ROWS, LANES), BF16)


def _prompt_attn_body(q_ref, k_ref, v_ref, o_ref, s_scr, p_scr, m_scr, a_scr, acc_scr):
    i = pl.program_id(2)
    tq = q_ref.shape[2]
    m_scr[...] = jnp.full_like(m_scr, NEG)
    acc_scr[...] = jnp.zeros_like(acc_scr)

    tk = tq // 2

    def scores(j, half):
        start = pl.multiple_of(j * tq + half * tk, tk)
        for e in range(2):
            s_scr[half, e] = _dot_nt(k_ref[0, e, pl.ds(start, tk), :], q_ref[0, e])

    def consume(j, half, masked):
        for e in range(2):
            _softmax_keys_on_rows(s_scr.at[half, e], p_scr.at[half, e], m_scr.at[e], a_scr.at[e],
                                  half * tk if masked else None)
            acc_scr[e] = a_scr[e] * acc_scr[e] + _dot(v_ref[0, e, j, :, half * tk:(half + 1) * tk], p_scr[half, e])

    def past_block(j, _):
        scores(j, 1)
        consume(j, 0, False)
        scores(j + 1, 0)
        consume(j, 1, False)
        return 0

    scores(0, 0)
    lax.fori_loop(0, i, past_block, 0)
    scores(i, 1)
    consume(i, 0, True)
    consume(i, 1, True)
    halves = []
    for e in range(2):
        own, other = slice(e * HEAD_DIM, (e + 1) * HEAD_DIM), slice((1 - e) * HEAD_DIM, (2 - e) * HEAD_DIM)
        halves.append(acc_scr[e, own, :] / acc_scr[e, other, :])
    o_ref[0] = jnp.concatenate(halves, axis=0).T


def _prompt_attn(qh, kh, vh, tq):
    nb, n_heads, t, _ = qh.shape
    pair = lambda rows, imap: pl.BlockSpec((1, 2, rows, LANES), imap)
    return pl.pallas_call(
        _prompt_attn_body,
        grid=(nb, n_heads // 2, t // tq),
        in_specs=[pair(tq, lambda b, j, i: (b, j, i, 0)),
                  pair(t, lambda b, j, i: (b, j, 0, 0)),
                  pl.BlockSpec((1, 2, t // tq, LANES, tq), lambda b, j, i: (b, j, 0, 0, 0))],
        out_specs=pl.BlockSpec((1, tq, LANES), lambda b, j, i: (b, i, j)),
        out_shape=jax.ShapeDtypeStruct((nb, t, n_heads * HEAD_DIM), F32),
        scratch_shapes=[pltpu.VMEM((2, 2, tq // 2, tq), F32), pltpu.VMEM((2, 2, tq // 2, tq), BF16),
                        pltpu.VMEM((2, 1, tq), F32), pltpu.VMEM((2, 1, tq), F32), pltpu.VMEM((2, LANES, tq), F32)],
        compiler_params=pltpu.CompilerParams(dimension_semantics=("arbitrary", "arbitrary", "arbitrary"),
                                             vmem_limit_bytes=VMEM_LIMIT),
        name="prompt_attn",
    )(qh, kh, vh)


def _ffn_chunk(hb, wg, wu, wd, taps_g, taps_u, prev_g, prev_u):
    g = _dot(hb, wg)
    u = _dot(hb, wu)
    cg = taps_g[0] * prev_g[0] + taps_g[1] * prev_g[1] + taps_g[2] * g + taps_g[3]
    cu = taps_u[0] * prev_u[0] + taps_u[1] * prev_u[1] + taps_u[2] * u + taps_u[3]
    return g, u, _dot((_silu(cg) * cu).astype(BF16), wd)


def _prompt_out_body(x_ref, yab_ref, yc_ref, mod_ref, n2g_ref, mixg_ref, wout_ref, wup_ref, wd_ref,
                     taps_ref, fg_ref, o_ref, fst_ref, buf_scr, halo_scr, act_scr, *, final):
    t = pl.program_id(1)
    tm = x_ref.shape[1]
    ab_w = yab_ref.shape[2]
    n_chunks = wup_ref.shape[1] // (2 * MXU_N)

    @pl.when(t == 0)
    def _():
        halo_scr[...] = jnp.zeros_like(halo_scr)

    y_c = (_rms(yc_ref[0]) * mixg_ref[:, ab_w:]).astype(BF16)
    attn = _dot(yab_ref[0], wout_ref[0:ab_w, :]) + _dot(y_c, wout_ref[ab_w:, :])
    x1 = x_ref[0] + mod_ref[0, 2:3, :] * attn
    hb = _modulated_norm(x1, n2g_ref[...], mod_ref[0, 4:5, :], mod_ref[0, 3:4, :]).astype(BF16)

    for c in range(n_chunks):
        conv = []
        for gu in range(2):
            slot = gu * n_chunks + c
            buf = buf_scr.at[c % 2, gu]
            cur = _dot(hb, wup_ref[:, slot * MXU_N:(slot + 1) * MXU_N])
            buf[0:SUBLANES, :] = halo_scr[slot]
            buf[SUBLANES:SUBLANES + tm, :] = cur
            tp = taps_ref[slot]
            conv.append(tp[0:1] * buf[pl.ds(SUBLANES - 2, tm), :] + tp[1:2] * buf[pl.ds(SUBLANES - 1, tm), :]
                        + tp[2:3] * cur + tp[3:4])
            tail = buf[tm:tm + SUBLANES, :]
            halo_scr[slot] = tail
            fst_ref[0, slot] = tail
        act_scr[:, c * MXU_N:(c + 1) * MXU_N] = (_silu(conv[0]) * conv[1]).astype(BF16)

    x2 = x1 + mod_ref[0, 5:6, :] * _dot(act_scr[...], wd_ref[...])
    o_ref[0] = _rms(x2) * fg_ref[...] if final else x2


def _resident(shape):
    return pl.BlockSpec(shape, lambda *_: (0,) * len(shape), pipeline_mode=pl.Buffered(1))


def _prompt_out(x, yab, yc, mod, p, final_g, tm, final):
    nb, t, d = x.shape
    n_chunks = p["w_up"].shape[1] // (2 * MXU_N)
    body = functools.partial(_prompt_out_body, final=final)
    const = lambda *shape: pl.BlockSpec(shape, lambda b, i: (0,) * len(shape))
    return pl.pallas_call(
        body,
        grid=(nb, t // tm),
        in_specs=[pl.BlockSpec((1, tm, d), lambda b, i: (b, i, 0)),
                  pl.BlockSpec((1, tm, yab.shape[2]), lambda b, i: (b, i, 0)),
                  pl.BlockSpec((1, tm, yc.shape[2]), lambda b, i: (b, i, 0)),
                  pl.BlockSpec((1, 6, d), lambda b, i: (b, 0, 0)),
                  const(1, d), const(1, d),
                  _resident(p["w_out"].shape), _resident(p["w_up"].shape),
                  _resident(p["w_down_full"].shape), _resident(p["ffn_taps"].shape), const(1, d)],
        out_specs=[pl.BlockSpec((1, tm, d), lambda b, i: (b, i, 0)),
                   pl.BlockSpec((1, 2 * n_chunks, SUBLANES, MXU_N), lambda b, i: (b, 0, 0, 0))],
        out_shape=[jax.ShapeDtypeStruct((nb, t, d), F32),
                   jax.ShapeDtypeStruct((nb, 2 * n_chunks, SUBLANES, MXU_N), F32)],
        scratch_shapes=[pltpu.VMEM((2, 2, SUBLANES + tm, MXU_N), F32),
                        pltpu.VMEM((2 * n_chunks, SUBLANES, MXU_N), F32),
                        pltpu.VMEM((tm, n_chunks * MXU_N), BF16)],
        compiler_params=pltpu.CompilerParams(dimension_semantics=("arbitrary", "arbitrary"),
                                             vmem_limit_bytes=VMEM_LIMIT),
        name="prompt_out",
    )(x, yab, yc, mod, p["norm2_g"], p["mix_g"], p["w_out"], p["w_up"], p["w_down_full"],
      p["ffn_taps"], final_g)


def _sample_in_body(x_ref, mod_ref, n1g_ref, win_ref, wf_ref, bf_ref, alng_ref, alnb_ref, ws0_ref, bs0_ref,
                    cw_ref, cb_ref, clng_ref, clnb_ref, mixg_ref, st_ref,
                    q_ref, k_ref, v_ref, logf_ref, yab_ref, cv_ref, nst_ref, *, a_w, b_w, c_w, conv_w):
    d = x_ref.shape[1]
    n_heads = c_w // HEAD_DIM
    h = _modulated_norm(x_ref[...], n1g_ref[...], mod_ref[:, d:2 * d], mod_ref[:, 0:d])
    hb = h.astype(BF16)
    z = _dot(hb, win_ref[...])
    u = _gelu(z[:, 0:a_w])
    vln = _layernorm(_gelu(z[:, a_w:2 * a_w]), alng_ref[...], alnb_ref[...])
    cv_ref[...] = vln
    y_a = _rms(u * (ws0_ref[...] * vln + bs0_ref[...])) * mixg_ref[:, 0:a_w]
    yab_ref[:, 0:a_w] = y_a.astype(BF16)
    o = 2 * a_w
    glu = z[:, o:o + b_w] * _sigmoid(z[:, o + b_w:o + 2 * b_w])
    hist = (conv_w - 1) * b_w
    acc = cb_ref[...] + cw_ref[conv_w - 1:conv_w, :] * glu
    for j in range(conv_w - 1):
        acc = acc + cw_ref[j:j + 1, :] * st_ref[:, j * b_w:(j + 1) * b_w]
    y_b = _rms(_silu(_layernorm(acc, clng_ref[...], clnb_ref[...]))) * mixg_ref[:, a_w:a_w + b_w]
    yab_ref[:, a_w:a_w + b_w] = y_b.astype(BF16)
    nst_ref[:, 0:hist - b_w] = st_ref[:, b_w:hist]
    nst_ref[:, hist - b_w:hist] = glu
    o += 2 * b_w
    q_ref[...] = z[:, o:o + c_w]
    k_ref[...] = z[:, o + c_w:o + 2 * c_w]
    v_ref[...] = z[:, o + 2 * c_w:o + 3 * c_w]
    logf = _log_sigmoid(_dot(hb, wf_ref[...]) + bf_ref[...])
    logf_ref[...] = logf[:, 0:n_heads]


def _sample_in(x, mod, state, p):
    n, d = x.shape
    a_w, b_w, c_w = p["a_w"], p["b_w"], p["c_w"]
    n_heads = c_w // HEAD_DIM
    conv_w = p["conv_w"].shape[0]
    body = functools.partial(_sample_in_body, a_w=a_w, b_w=b_w, c_w=c_w, conv_w=conv_w)
    outs = [((n, c_w), F32), ((n, c_w), F32), ((n, c_w), F32), ((n, n_heads), F32),
            ((n, a_w + b_w), BF16), ((n, a_w), F32), (state.shape, F32)]
    return pl.pallas_call(
        body,
        out_shape=[jax.ShapeDtypeStruct(s, dt) for s, dt in outs],
        compiler_params=pltpu.CompilerParams(vmem_limit_bytes=VMEM_LIMIT),
        name="sample_in",
    )(x, mod, p["norm1_g"], p["w_in"], p["w_f"], p["b_f"], p["a_ln_g"], p["a_ln_b"], p["w_s0"], p["b_s0"],
      p["conv_w"], p["conv_b"], p["conv_ln_g"], p["conv_ln_b"], p["mix_g"], state)


def _sample_attn_body(pt_ref, q_ref, kn_ref, vn_ref, fn_ref, *refs, n_pages):
    del pt_ref
    k_refs, v_refs, f_refs = refs[:n_pages], refs[n_pages:2 * n_pages], refs[2 * n_pages:3 * n_pages]
    o_ref, qb_scr, m_scr, l_scr, c_scr, s_scr, p_scr, acc_scr = refs[3 * n_pages:]
    gi = pl.program_id(1)
    n_heads = qb_scr.shape[0]

    def pair_column(row_ref, pair):
        return _row_to_col(row_ref[0][:, pair * LANES:(pair + 1) * LANES])

    @pl.when(gi == 0)
    def _():
        for pair in range(n_heads // 2):
            q_col = pair_column(q_ref, pair) * SCALE
            for e in range(2):
                qb_scr[2 * pair + e] = jnp.broadcast_to(q_col[e * HEAD_DIM:(e + 1) * HEAD_DIM], (HEAD_DIM, PAGE))
        m_scr[...] = jnp.full_like(m_scr, NEG)
        l_scr[...] = jnp.zeros_like(l_scr)
        acc_scr[...] = jnp.zeros_like(acc_scr)
        c_scr[...] = _row_to_col(fn_ref[0])

    r = lax.broadcasted_iota(jnp.int32, (PAGE, PAGE), 0)
    c = lax.broadcasted_iota(jnp.int32, (PAGE, PAGE), 1)
    later = jnp.where(r > c, 1.0, 0.0).astype(BF16)
    logf = jnp.concatenate([f_refs[i][0, 0] for i in range(n_pages)], axis=0)
    within = _dot3_r(logf, later)
    page_total = within[:, 0:1] + logf[:, 0:1]
    carry = c_scr[...]
    after = [None] * n_pages
    for i in reversed(range(n_pages)):
        after[i] = carry
        carry = carry + page_total[i * n_heads:(i + 1) * n_heads]
        for h in range(n_heads):
            s_scr[pl.ds(i * n_heads + h, 1), :] = jnp.sum(k_refs[i][0, 0, h] * qb_scr[h], axis=0, keepdims=True)
    c_scr[...] = carry
    scores = s_scr[...] + within + jnp.concatenate(after, axis=0)

    m_old = m_scr[...]
    row_max = jnp.max(scores, axis=1, keepdims=True)
    m_new = m_old
    for i in range(n_pages):
        m_new = jnp.maximum(m_new, row_max[i * n_heads:(i + 1) * n_heads])
    alpha = jnp.exp(m_old - m_new)
    pr = jnp.exp(scores - jnp.concatenate([m_new] * n_pages, axis=0))
    p_scr[...] = pr
    row_sum = jnp.sum(pr, axis=1, keepdims=True)
    l = alpha * l_scr[...]
    for i in range(n_pages):
        l = l + row_sum[i * n_heads:(i + 1) * n_heads]
    m_scr[...] = m_new
    l_scr[...] = l
    for h in range(n_heads):
        acc = alpha[h:h + 1, :] * acc_scr[h]
        for i in range(n_pages):
            acc = acc + p_scr[pl.ds(i * n_heads + h, 1), :] * v_refs[i][0, 0, h]
        acc_scr[h] = acc

    @pl.when(gi == pl.num_programs(1) - 1)
    def _():
        for pair in range(n_heads // 2):
            q_col, k_col, v_col = (pair_column(r, pair) for r in (q_ref, kn_ref, vn_ref))
            outs = []
            for e in range(2):
                h = 2 * pair + e
                own = slice(e * HEAD_DIM, (e + 1) * HEAD_DIM)
                s_new = jnp.sum(q_col[own] * SCALE * k_col[own], axis=0, keepdims=True)
                m_h = m_new[h:h + 1, :]
                m_f = jnp.maximum(m_h, s_new)
                a_f = jnp.exp(m_h - m_f)
                p_new = jnp.exp(s_new - m_f)
                l_f = a_f * l[h:h + 1, :] + p_new
                past = jnp.sum(acc_scr[h], axis=1, keepdims=True)
                outs.append((a_f * past + p_new * v_col[own]) / l_f)
            o_ref[0, :, pair * LANES:(pair + 1) * LANES] = _col_to_row(jnp.concatenate(outs, axis=0))


def _sample_attn(q, k_new, v_new, logf_new, cache_k, cache_v, cache_logf, page_table, layer):
    n, width = q.shape
    n_heads = width // HEAD_DIM
    n_seq_pages = page_table.shape[1]
    pps = min(PAGES_PER_STEP, n_seq_pages)
    n_steps = n_seq_pages // pps
    kt = cache_k.transpose(0, 1, 3, 4, 2)
    vt = cache_v.transpose(0, 1, 3, 4, 2)
    ft = cache_logf.transpose(0, 1, 3, 2)

    def page(i, *tail):
        return pl.BlockSpec((1, 1) + tail,
                            lambda b, g, pt: (layer, pt[b, (n_steps - 1 - g) * pps + i]) + (0,) * len(tail))

    row = lambda last: pl.BlockSpec((1, 1, last), lambda b, g, pt: (b, 0, 0))
    grid_spec = pltpu.PrefetchScalarGridSpec(
        num_scalar_prefetch=1,
        grid=(n, n_steps),
        in_specs=[row(width), row(width), row(width), row(n_heads)]
                 + [page(i, n_heads, HEAD_DIM, PAGE) for i in range(pps)] * 2
                 + [page(i, n_heads, PAGE) for i in range(pps)],
        out_specs=row(width),
        scratch_shapes=[pltpu.VMEM((n_heads, HEAD_DIM, PAGE), F32),
                        pltpu.VMEM((n_heads, 1), F32), pltpu.VMEM((n_heads, 1), F32), pltpu.VMEM((n_heads, 1), F32),
                        pltpu.VMEM((pps * n_heads, PAGE), F32), pltpu.VMEM((pps * n_heads, PAGE), F32),
                        pltpu.VMEM((n_heads, HEAD_DIM, PAGE), F32)],
    )
    as_rows = lambda a: a.reshape(n, 1, a.shape[1])
    out = pl.pallas_call(
        functools.partial(_sample_attn_body, n_pages=pps),
        grid_spec=grid_spec,
        out_shape=jax.ShapeDtypeStruct((n, 1, width), F32),
        compiler_params=pltpu.CompilerParams(dimension_semantics=("arbitrary", "arbitrary"),
                                             vmem_limit_bytes=VMEM_LIMIT),
        name="sample_attn",
    )(page_table, as_rows(q), as_rows(k_new), as_rows(v_new), as_rows(logf_new),
      *([kt] * pps), *([vt] * pps), *([ft] * pps))
    return out.reshape(n, width)


def _sample_out_body(x_ref, yab_ref, yc_ref, mod_ref, n2g_ref, mixg_ref, wout_ref, wg_ref, wu_ref, wd_ref,
                     taps_g_ref, taps_u_ref, s0g_ref, s0u_ref, s1g_ref, s1u_ref, fg_ref,
                     o_ref, ng_ref, nu_ref, pg_ref, pu_ref, x1_scr, hb_scr, acc_scr, *, final):
    c = pl.program_id(0)
    d = x_ref.shape[1]
    ab_w = yab_ref.shape[1]

    @pl.when(c == 0)
    def _():
        y_c = (_rms(yc_ref[...]) * mixg_ref[:, ab_w:]).astype(BF16)
        attn = _dot(yab_ref[...], wout_ref[0:ab_w, :]) + _dot(y_c, wout_ref[ab_w:, :])
        x1 = x_ref[...] + mod_ref[:, 2 * d:3 * d] * attn
        x1_scr[...] = x1
        hb_scr[...] = _modulated_norm(x1, n2g_ref[...], mod_ref[:, 4 * d:5 * d], mod_ref[:, 3 * d:4 * d]).astype(BF16)
        acc_scr[...] = jnp.zeros_like(acc_scr)

    tg, tu = taps_g_ref[0], taps_u_ref[0]
    g, u, down = _ffn_chunk(hb_scr[...], wg_ref[...], wu_ref[...], wd_ref[...],
                            [tg[i:i + 1] for i in range(4)], [tu[i:i + 1] for i in range(4)],
                            (s0g_ref[...], s1g_ref[...]), (s0u_ref[...], s1u_ref[...]))
    ng_ref[...] = g
    nu_ref[...] = u
    pg_ref[...] = s1g_ref[...]
    pu_ref[...] = s1u_ref[...]
    acc_scr[...] += down

    @pl.when(c == pl.num_programs(0) - 1)
    def _():
        x2 = x1_scr[...] + mod_ref[:, 5 * d:6 * d] * acc_scr[...]
        o_ref[...] = _rms(x2) * fg_ref[...] if final else x2


def _sample_out(x, yab, yc, mod, state, p, final_g, final):
    n, d = x.shape
    n_chunks = p["w_up"].shape[1] // (2 * MXU_N)
    d_ff = n_chunks * MXU_N
    const = lambda *shape: pl.BlockSpec(shape, lambda c: (0,) * len(shape))
    st = lambda off: pl.BlockSpec((n, MXU_N), lambda c: (0, off + c))
    col = pl.BlockSpec((n, MXU_N), lambda c: (0, c))
    outs = pl.pallas_call(
        functools.partial(_sample_out_body, final=final),
        grid=(n_chunks,),
        in_specs=[const(n, d), const(*yab.shape), const(*yc.shape), const(*mod.shape), const(1, d), const(1, d),
                  const(*p["w_out"].shape),
                  pl.BlockSpec((d, MXU_N), lambda c: (0, c)),
                  pl.BlockSpec((d, MXU_N), lambda c: (0, n_chunks + c)),
                  pl.BlockSpec((MXU_N, d), lambda c: (c, 0)),
                  pl.BlockSpec((1, SUBLANES, MXU_N), lambda c: (c, 0, 0)),
                  pl.BlockSpec((1, SUBLANES, MXU_N), lambda c: (n_chunks + c, 0, 0)),
                  st(0), st(n_chunks), st(2 * n_chunks), st(3 * n_chunks), const(1, d)],
        out_specs=[const(n, d), col, col, col, col],
        out_shape=[jax.ShapeDtypeStruct((n, d), F32)] + [jax.ShapeDtypeStruct((n, d_ff), F32)] * 4,
        scratch_shapes=[pltpu.VMEM((n, d), F32), pltpu.VMEM((n, d), BF16), pltpu.VMEM((n, d), F32)],
        compiler_params=pltpu.CompilerParams(dimension_semantics=("arbitrary",), vmem_limit_bytes=VMEM_LIMIT),
        name="sample_out",
    )(x, yab, yc, mod, p["norm2_g"], p["mix_g"], p["w_out"], p["w_up"], p["w_up"], p["w_down_full"],
      p["ffn_taps"], p["ffn_taps"], state, state, state, state, final_g)
    x2, new_g, new_u, prev_g, prev_u = outs
    new_state = jnp.stack([jnp.concatenate([prev_g, prev_u], axis=1), jnp.concatenate([new_g, new_u], axis=1)], axis=1)
    return x2, new_state


def _layer_params(l, d, norm1_g, norm2_g, w_in, b_forget, a_ln_g, a_ln_b, w_s, b_s, conv_w, conv_b,
                  conv_ln_g, conv_ln_b, mix_g, w_out, w_up, ffn_conv_w, ffn_conv_b, w_down):
    a_w, b_w, n_heads = a_ln_g.shape[1], conv_b.shape[1], b_forget.shape[1]
    c_w = n_heads * HEAD_DIM
    main = 2 * a_w + 2 * b_w + 3 * c_w
    d_ff = w_down.shape[1]
    n_chunks = d_ff // MXU_N
    row = lambda v: v.reshape(1, -1)
    pad_lanes = lambda a: jnp.pad(a, ((0, 0), (0, LANES - a.shape[1])))
    decay_q, decay_k = _decay_placement(n_heads)
    chunked = lambda v: v.reshape(2 * n_chunks, MXU_N)
    taps = jnp.stack([chunked(ffn_conv_w[l, 0]), chunked(ffn_conv_w[l, 1]), chunked(ffn_conv_w[l, 2]),
                      chunked(ffn_conv_b[l])], axis=1)
    taps = jnp.pad(taps, ((0, 0), (0, SUBLANES - 4), (0, 0)))
    return dict(
        a_w=a_w, b_w=b_w, c_w=c_w,
        norm1_g=row(norm1_g[l]), norm2_g=row(norm2_g[l]),
        w_in=w_in[l, :, :main].astype(BF16),
        w_vt=w_in[l, :, main - c_w:main].T.astype(BF16),
        w_f=pad_lanes(jnp.tile(w_in[l, :, main:], (1, 3))).astype(BF16),
        b_f=pad_lanes(jnp.tile(row(b_forget[l]), (1, 3))),
        decay_q=decay_q, decay_k=decay_k,
        a_ln_g=row(a_ln_g[l]), a_ln_b=row(a_ln_b[l]),
        w_s=w_s[l], b_s_full=jnp.repeat(b_s[l].T, HEAD_DIM, axis=1),
        w_s0=row(jnp.repeat(w_s[l, :, 0, 0], HEAD_DIM)), b_s0=row(jnp.repeat(b_s[l, :, 0], HEAD_DIM)),
        conv_w=conv_w[l], conv_b=row(conv_b[l]), conv_ln_g=row(conv_ln_g[l]), conv_ln_b=row(conv_ln_b[l]),
        mix_g=row(mix_g[l]),
        w_out=w_out[l].astype(BF16),
        w_up=w_up[l].astype(BF16),
        w_down_full=w_down[l].astype(BF16),
        ffn_taps=taps,
    )


def kernel(x_prompt, x_sample, cache_k, cache_v, cache_logf, state_conv, state_ffn_conv, page_table, c_prompt, c_sample, norm1_g, norm2_g, w_ada, b_ada, w_in, b_forget, a_ln_g, a_ln_b, w_s, b_s, conv_w, conv_b, conv_ln_g, conv_ln_b, mix_g, w_out, w_up, ffn_conv_w, ffn_conv_b, w_down, final_g):
    nb, t, d = x_prompt.shape
    ns = x_sample.shape[0]
    depth = w_in.shape[0]
    n_heads = b_forget.shape[1]
    tm = min(ROW_TILE, t)

    c_all = jnp.concatenate([c_prompt, c_sample], axis=0)
    pad_rows = (-c_all.shape[0]) % 16
    mod = _ada(jnp.pad(c_all, ((0, pad_rows), (0, 0))), w_ada, b_ada)
    fg = final_g.reshape(1, d)

    xp, xs = x_prompt, x_sample.reshape(ns, d)
    prompt_out, sample_out = [], []
    for l in range(depth):
        p = _layer_params(l, d, norm1_g, norm2_g, w_in, b_forget, a_ln_g, a_ln_b, w_s, b_s, conv_w, conv_b,
                          conv_ln_g, conv_ln_b, mix_g, w_out, w_up, ffn_conv_w, ffn_conv_b, w_down)
        n_chunks = p["w_up"].shape[1] // (2 * MXU_N)
        final = l == depth - 1
        mod_p = mod[l, :nb].reshape(nb, 6, d)
        mod_s = mod[l, nb:nb + ns]

        k, vt, logf, qh, kh, vh, yab, conv_st = _prompt_in(xp, mod_p, p, tm)
        yc = _prompt_attn(qh, kh, vh, tm)
        xp, ffn_tail = _prompt_out(xp, yab, yc, mod_p, p, fg, tm, final)
        ffn_st = ffn_tail[:, :, SUBLANES - 2:, :].transpose(0, 2, 1, 3).reshape(nb, 2, 2 * n_chunks * MXU_N)
        v_rows = vt.reshape(nb, n_heads, HEAD_DIM, t).transpose(0, 3, 1, 2)
        prompt_out.append((k.reshape(nb, t, n_heads, HEAD_DIM), v_rows, logf, conv_st, ffn_st))

        st_conv = state_conv[l].reshape(ns, -1)
        q_s, k_s, v_s, logf_s, yab_s, chunk_v, new_conv = _sample_in(xs, mod_s, st_conv, p)
        yc_s = _sample_attn(q_s, k_s, v_s, logf_s, cache_k, cache_v, cache_logf, page_table, l)
        xs, new_ffn = _sample_out(xs, yab_s, yc_s, mod_s, state_ffn_conv[l].reshape(ns, -1), p, fg, final)
        sample_out.append((k_s.reshape(ns, 1, n_heads, HEAD_DIM), v_s.reshape(ns, 1, n_heads, HEAD_DIM),
                           logf_s.reshape(ns, 1, n_heads), new_conv.reshape(state_conv.shape[1:]), new_ffn,
                           chunk_v.reshape(ns, 1, -1)))

    stack = lambda outs, i: jnp.stack([o[i] for o in outs])
    return (xp, xs.reshape(ns, 1, d),
            stack(prompt_out, 0), stack(prompt_out, 1), stack(prompt_out, 2), stack(prompt_out, 3), stack(prompt_out, 4),
            stack(sample_out, 0), stack(sample_out, 1), stack(sample_out, 2), stack(sample_out, 3), stack(sample_out, 4),
            stack(sample_out, 5))
```

```python
import functools

import numpy as np

import jax
import jax.numpy as jnp
from jax import lax
from jax.experimental import pallas as pl
from jax.experimental.pallas import tpu as pltpu

F32 = jnp.float32
BF16 = jnp.bfloat16

HEAD_DIM = 64
CHUNK = 128
PAGE = 128
EPS = 1e-6
SCALE = HEAD_DIM ** -0.5
NEG = -1e30
LOG2E = 1.4426950408889634

LANES = 128
SUBLANES = 8
MXU_N = 256
VMEM_LIMIT = 56 * 1024 * 1024

ROW_TILE = 512
CONV_ROWS = 64
ATTN_ROWS = 32
HALO = 32
PAGES_PER_STEP = 16


def _dot(a, b):
    return jnp.dot(a, b, preferred_element_type=F32)


def _dot_nt(a, b):
    return lax.dot_general(a, b, (((1,), (1,)), ((), ())), preferred_element_type=F32)


def _rms(x):
    return x * lax.rsqrt(jnp.mean(x * x, axis=-1, keepdims=True) + EPS)


def _layernorm(x, g, b):
    xc = x - jnp.mean(x, axis=-1, keepdims=True)
    return xc * lax.rsqrt(jnp.mean(xc * xc, axis=-1, keepdims=True) + EPS) * g + b


def _gelu(x):
    return 0.5 * x * (1.0 + lax.erf(x * (2.0 ** -0.5)))


def _sigmoid(x):
    return 1.0 / (1.0 + jnp.exp(-x))


def _silu(x):
    return x * _sigmoid(x)


def _log_sigmoid(x):
    return jnp.minimum(x, 0.0) - jnp.log(1.0 + jnp.exp(-jnp.abs(x)))


def _split3(x):
    hi = x.astype(BF16)
    r = x - hi.astype(F32)
    mid = r.astype(BF16)
    lo = (r - mid.astype(F32)).astype(BF16)
    return hi, mid, lo


def _dot3(w01, x):
    hi, mid, lo = _split3(x)
    return _dot(w01, hi) + _dot(w01, mid) + _dot(w01, lo)


def _dot3_r(x, w01):
    hi, mid, lo = _split3(x)
    return _dot(hi, w01) + _dot(mid, w01) + _dot(lo, w01)


def _diagonal(vec, n):
    r = lax.broadcasted_iota(jnp.int32, (n, n), 0)
    c = lax.broadcasted_iota(jnp.int32, (n, n), 1)
    return jnp.where(r == c, jnp.broadcast_to(vec, (n, n)), 0.0)


def _row_to_col(row):
    return jnp.sum(_diagonal(row, row.shape[1]), axis=1, keepdims=True)


def _col_to_row(col):
    return jnp.sum(_diagonal(col, col.shape[0]), axis=0, keepdims=True)


def _modulated_norm(x, g, scale, shift):
    return _rms(x) * g * (1.0 + scale) + shift


def _ada_body(c_ref, w_ref, b_ref, o_ref):
    c = c_ref[...]
    o_ref[0] = _dot(_silu(c).astype(BF16), w_ref[0].astype(BF16)) + b_ref[0]


def _ada(c_all, w_ada, b_ada):
    depth, d, n = w_ada.shape
    rows = c_all.shape[0]
    tn = n // 4
    return pl.pallas_call(
        _ada_body,
        grid=(depth, n // tn),
        in_specs=[pl.BlockSpec((rows, d), lambda l, j: (0, 0)),
                  pl.BlockSpec((1, d, tn), lambda l, j: (l, 0, j)),
                  pl.BlockSpec((1, 1, tn), lambda l, j: (l, 0, j))],
        out_specs=pl.BlockSpec((1, rows, tn), lambda l, j: (l, 0, j)),
        out_shape=jax.ShapeDtypeStruct((depth, rows, n), F32),
        compiler_params=pltpu.CompilerParams(dimension_semantics=("arbitrary", "arbitrary"),
                                             vmem_limit_bytes=VMEM_LIMIT),
        name="ada_mod",
    )(c_all, w_ada, b_ada.reshape(depth, 1, n))


def _own_half(rows, h):
    lane = lax.broadcasted_iota(jnp.int32, (rows, LANES), 1)
    parity = h % 2
    return lane, (lane >= HEAD_DIM * parity) & (lane < HEAD_DIM * (parity + 1))


def _decay_placement(n_heads):
    pq = np.zeros((LANES, n_heads * LANES), np.float32)
    pk = np.zeros((LANES, n_heads * LANES), np.float32)
    one_lane = 3 * n_heads
    for h in range(n_heads):
        base = h * LANES + HEAD_DIM * (1 - h % 2)
        for i in range(3):
            pq[i * n_heads + h, base + i] = 1.0
            pq[one_lane, base + 3 + i] = 1.0
            pk[one_lane, base + i] = 1.0
            pk[i * n_heads + h, base + 3 + i] = -1.0
    return jnp.asarray(pq, BF16), jnp.asarray(pk, BF16)


def _prompt_in_body(x_ref, mod_ref, n1g_ref, win_ref, wvt_ref, wf_ref, bf_ref, alng_ref, alnb_ref, ws_ref, bsf_ref,
                    cw_ref, cb_ref, clng_ref, clnb_ref, mixg_ref, tri_ref, pq_ref, pk_ref,
                    k_ref, vt_ref, logf_ref, qh_ref, kh_ref, vh_ref, yab_ref, cst_ref,
                    xp_scr, xs_scr, carry_scr, *, a_w, b_w, c_w, conv_w):
    t = pl.program_id(1)
    tm = x_ref.shape[1]
    n_heads = c_w // HEAD_DIM

    @pl.when(t == 0)
    def _():
        xp_scr[0:HALO, :] = jnp.zeros((HALO, b_w), F32)
        carry_scr[...] = jnp.zeros_like(carry_scr)

    x = x_ref[0]
    h = _modulated_norm(x, n1g_ref[...], mod_ref[0, 1:2, :], mod_ref[0, 0:1, :])
    hb = h.astype(BF16)

    z_uv = _dot(hb, win_ref[:, 0:2 * a_w])
    u = _gelu(z_uv[:, :a_w])
    vln = _layernorm(_gelu(z_uv[:, a_w:]), alng_ref[...], alnb_ref[...]).astype(BF16)
    row = lax.broadcasted_iota(jnp.int32, (CHUNK, CHUNK), 0)
    col = lax.broadcasted_iota(jnp.int32, (CHUNK, CHUNK), 1)
    lane_head = lax.broadcasted_iota(jnp.int32, (CHUNK, a_w), 1) // HEAD_DIM
    w_tril = [jnp.where(col <= row, ws_ref[hh], 0.0).astype(BF16) for hh in range(a_w // HEAD_DIM)]
    sv_chunks = []
    for c in range(tm // CHUNK):
        vc = vln[c * CHUNK:(c + 1) * CHUNK]
        sv = bsf_ref[...]
        for hh, w in enumerate(w_tril):
            sv = sv + jnp.where(lane_head == hh, _dot(w, vc), 0.0)
        sv_chunks.append(sv)
    y_a = _rms(u * jnp.concatenate(sv_chunks, axis=0)) * mixg_ref[:, 0:a_w]
    yab_ref[0, :, 0:a_w] = y_a.astype(BF16)

    z_ag = _dot(hb, win_ref[:, 2 * a_w:2 * a_w + 2 * b_w])
    xp_scr[HALO:HALO + tm, :] = z_ag[:, :b_w] * _sigmoid(z_ag[:, b_w:])
    shifted_rows = xs_scr.shape[1]
    for r in range(1, SUBLANES):
        xs_scr[r - 1] = xp_scr[pl.ds(r, shifted_rows), :]
    first_tap = HALO - (conv_w - 1)
    conv_blocks = []
    for rb in range(tm // CONV_ROWS):
        acc = jnp.broadcast_to(cb_ref[...], (CONV_ROWS, b_w))
        for j in range(conv_w):
            whole, r = divmod(first_tap + j, SUBLANES)
            src = xp_scr if r == 0 else xs_scr.at[r - 1]
            acc = acc + cw_ref[j:j + 1, :] * src[pl.ds(rb * CONV_ROWS + whole * SUBLANES, CONV_ROWS), :]
        conv_blocks.append(acc)
    conv = jnp.concatenate(conv_blocks, axis=0)
    y_b = _rms(_silu(_layernorm(conv, clng_ref[...], clnb_ref[...]))) * mixg_ref[:, a_w:a_w + b_w]
    yab_ref[0, :, a_w:a_w + b_w] = y_b.astype(BF16)
    cst_ref[0] = xp_scr[pl.ds(HALO + tm - (conv_w - 1), conv_w - 1), :]
    xp_scr[0:HALO, :] = xp_scr[tm:tm + HALO, :]

    o = 2 * a_w + 2 * b_w
    z_q = _dot(hb, win_ref[:, o:o + c_w]) * (SCALE * LOG2E)
    z_k = _dot(hb, win_ref[:, o + c_w:o + 2 * c_w])
    z_vt = _dot_nt(wvt_ref[...], hb)
    k_ref[0] = z_k
    vt_ref[0] = z_vt
    logf = _log_sigmoid(_dot(hb, wf_ref[...]) + bf_ref[...])
    logf_ref[0] = logf[:, 0:n_heads]
    cum = _dot3(tri_ref[...], logf) + carry_scr[...]
    carry_scr[...] = cum[tm - 1:tm, :]
    hi, mid, lo = _split3(cum * LOG2E)
    lane = lax.broadcasted_iota(jnp.int32, (tm, LANES), 1)
    one = jnp.where(lane == 3 * n_heads, 1.0, 0.0).astype(BF16)
    packed = jnp.where(lane < n_heads, hi, jnp.where(lane < 2 * n_heads, mid, jnp.where(lane < 3 * n_heads, lo, one)))
    decay_q = _dot(packed, pq_ref[...])
    decay_k = _dot(packed, pk_ref[...])
    for hh in range(n_heads):
        slab = slice((hh // 2) * LANES, (hh // 2 + 1) * LANES)
        mine = slice(hh * LANES, (hh + 1) * LANES)
        own = _own_half(tm, hh)[1]
        qh_ref[0, hh] = jnp.where(own, z_q[:, slab], decay_q[:, mine]).astype(BF16)
        kh_ref[0, hh] = jnp.where(own, z_k[:, slab], decay_k[:, mine]).astype(BF16)
        feat = lax.broadcasted_iota(jnp.int32, (LANES, tm), 0) // HEAD_DIM
        vh_ref[0, hh, 0] = jnp.where(feat == hh % 2, z_vt[slab, :], 1.0).astype(BF16)


def _prompt_in(x, mod, p, tm):
    nb, t, d = x.shape
    a_w, b_w, c_w = p["a_w"], p["b_w"], p["c_w"]
    n_heads = c_w // HEAD_DIM
    conv_w = p["conv_w"].shape[0]
    const = lambda *shape: pl.BlockSpec(shape, lambda b, i: (0,) * len(shape))
    body = functools.partial(_prompt_in_body, a_w=a_w, b_w=b_w, c_w=c_w, conv_w=conv_w)
    return pl.pallas_call(
        body,
        grid=(nb, t // tm),
        in_specs=[pl.BlockSpec((1, tm, d), lambda b, i: (b, i, 0)),
                  pl.BlockSpec((1, 6, d), lambda b, i: (b, 0, 0)),
                  const(1, d), _layer_spec(p["w_in"], p["layer"]), _layer_spec(p["w_vt"], p["layer"]),
                  const(d, LANES), const(1, LANES),
                  const(1, a_w), const(1, a_w), const(*p["w_s"].shape), const(CHUNK, a_w),
                  const(*p["conv_w"].shape), const(1, b_w), const(1, b_w), const(1, b_w), const(1, d),
                  const(tm, tm), const(*p["decay_q"].shape), const(*p["decay_k"].shape)],
        out_specs=[pl.BlockSpec((1, tm, c_w), lambda b, i: (b, i, 0)),
                   pl.BlockSpec((1, c_w, tm), lambda b, i: (b, 0, i)),
                   pl.BlockSpec((1, tm, n_heads), lambda b, i: (b, i, 0)),
                   pl.BlockSpec((1, n_heads, tm, LANES), lambda b, i: (b, 0, i, 0)),
                   pl.BlockSpec((1, n_heads, tm, LANES), lambda b, i: (b, 0, i, 0)),
                   pl.BlockSpec((1, n_heads, 1, LANES, tm), lambda b, i: (b, 0, i, 0, 0)),
                   pl.BlockSpec((1, tm, a_w + b_w), lambda b, i: (b, i, 0)),
                   pl.BlockSpec((1, conv_w - 1, b_w), lambda b, i: (b, 0, 0))],
        out_shape=[jax.ShapeDtypeStruct((nb, t, c_w), F32),
                   jax.ShapeDtypeStruct((nb, c_w, t), F32),
                   jax.ShapeDtypeStruct((nb, t, n_heads), F32),
                   jax.ShapeDtypeStruct((nb, n_heads, t, LANES), BF16),
                   jax.ShapeDtypeStruct((nb, n_heads, t, LANES), BF16),
                   jax.ShapeDtypeStruct((nb, n_heads, t // tm, LANES, tm), BF16),
                   jax.ShapeDtypeStruct((nb, t, a_w + b_w), BF16),
                   jax.ShapeDtypeStruct((nb, conv_w - 1, b_w), F32)],
        scratch_shapes=[pltpu.VMEM((HALO + tm, b_w), F32),
                        pltpu.VMEM((SUBLANES - 1, HALO + tm - SUBLANES, b_w), F32),
                        pltpu.VMEM((1, LANES), F32)],
        compiler_params=pltpu.CompilerParams(dimension_semantics=("arbitrary", "arbitrary"),
                                             vmem_limit_bytes=VMEM_LIMIT),
        name="prompt_in",
    )(x, mod, p["norm1_g"], p["w_in"], p["w_vt"], p["w_f"], p["b_f"], p["a_ln_g"], p["a_ln_b"], p["w_s"],
      p["b_s_full"], p["conv_w"], p["conv_b"], p["conv_ln_g"], p["conv_ln_b"], p["mix_g"],
      jnp.tril(jnp.ones((tm, tm), BF16)), p["decay_q"], p["decay_k"])


def _softmax_keys_on_rows(s_ref, p_ref, m_ref, a_ref, first_key):
    tk, tq = s_ref.shape

    def visible(r0, c0):
        if first_key is None or first_key + r0 + ATTN_ROWS - 1 <= c0:
            return "all"
        return "none" if first_key + r0 > c0 + LANES - 1 else "some"

    def chunk(r0, c0):
        sc = s_ref[r0:r0 + ATTN_ROWS, c0:c0 + LANES]
        if visible(r0, c0) == "some":
            key = first_key + r0 + lax.broadcasted_iota(jnp.int32, (ATTN_ROWS, LANES), 0)
            qry = c0 + lax.broadcasted_iota(jnp.int32, (ATTN_ROWS, LANES), 1)
            sc = jnp.where(key <= qry, sc, NEG)
        return sc

    for c0 in range(0, tq, LANES):
        cols = slice(c0, c0 + LANES)
        seen = [r0 for r0 in range(0, tk, ATTN_ROWS) if visible(r0, c0) != "none"]
        if not seen:
            a_ref[:, cols] = jnp.ones((1, LANES), F32)
            p_ref[:, cols] = jnp.zeros((tk, LANES), BF16)
            continue
        top = chunk(seen[0], c0)
        for r0 in seen[1:]:
            top = jnp.maximum(top, chunk(r0, c0))
        m_old = m_ref[:, cols]
        m_new = jnp.maximum(m_old, jnp.max(top, axis=0, keepdims=True))
        m_ref[:, cols] = m_new
        a_ref[:, cols] = jnp.exp2(m_old - m_new)
        for r0 in range(0, tk, ATTN_ROWS):
            if r0 in seen:
                p_ref[r0:r0 + ATTN_ROWS, cols] = jnp.exp2(chunk(r0, c0) - m_new).astype(BF16)
            else:
                p_ref[r0:r0 + ATTN_ROWS, cols] = jnp.zeros((ATTN_ROWS, LANES), BF16)


def _prompt_attn_body(q_ref, k_ref, v_ref, o_ref, s_scr, p_scr, m_scr, a_scr, acc_scr):
    i = pl.program_id(2)
    tq = q_ref.shape[2]
    m_scr[...] = jnp.full_like(m_scr, NEG)
    acc_scr[...] = jnp.zeros_like(acc_scr)

    tk = tq // 2

    def scores(j, half):
        start = pl.multiple_of(j * tq + half * tk, tk)
        for e in range(2):
            s_scr[half, e] = _dot_nt(k_ref[0, e, pl.ds(start, tk), :], q_ref[0, e])

    def consume(j, half, masked):
        for e in range(2):
            _softmax_keys_on_rows(s_scr.at[half, e], p_scr.at[half, e], m_scr.at[e], a_scr.at[e],
                                  half * tk if masked else None)
            acc_scr[e] = a_scr[e] * acc_scr[e] + _dot(v_ref[0, e, j, :, half * tk:(half + 1) * tk], p_scr[half, e])

    def past_block(j):
        scores(j, 1)
        consume(j, 0, False)
        scores(j + 1, 0)
        consume(j, 1, False)

    def past_pair(jj, _):
        past_block(2 * jj)
        past_block(2 * jj + 1)
        return 0

    scores(0, 0)
    lax.fori_loop(0, i // 2, past_pair, 0)

    @pl.when(i % 2 == 1)
    def _():
        past_block(i - 1)

    scores(i, 1)
    consume(i, 0, True)
    consume(i, 1, True)
    halves = []
    for e in range(2):
        own, other = slice(e * HEAD_DIM, (e + 1) * HEAD_DIM), slice((1 - e) * HEAD_DIM, (2 - e) * HEAD_DIM)
        halves.append(acc_scr[e, own, :] / acc_scr[e, other, :])
    o_ref[0] = jnp.concatenate(halves, axis=0).T


def _prompt_attn(qh, kh, vh, tq):
    nb, n_heads, t, _ = qh.shape
    pair = lambda rows, imap: pl.BlockSpec((1, 2, rows, LANES), imap)
    return pl.pallas_call(
        _prompt_attn_body,
        grid=(nb, n_heads // 2, t // tq),
        in_specs=[pair(tq, lambda b, j, i: (b, j, i, 0)),
                  pair(t, lambda b, j, i: (b, j, 0, 0)),
                  pl.BlockSpec((1, 2, t // tq, LANES, tq), lambda b, j, i: (b, j, 0, 0, 0))],
        out_specs=pl.BlockSpec((1, tq, LANES), lambda b, j, i: (b, i, j)),
        out_shape=jax.ShapeDtypeStruct((nb, t, n_heads * HEAD_DIM), F32),
        scratch_shapes=[pltpu.VMEM((2, 2, tq // 2, tq), F32), pltpu.VMEM((2, 2, tq // 2, tq), BF16),
                        pltpu.VMEM((2, 1, tq), F32), pltpu.VMEM((2, 1, tq), F32), pltpu.VMEM((2, LANES, tq), F32)],
        compiler_params=pltpu.CompilerParams(dimension_semantics=("arbitrary", "arbitrary", "arbitrary"),
                                             vmem_limit_bytes=VMEM_LIMIT),
        name="prompt_attn",
    )(qh, kh, vh)


def _ffn_chunk(hb, wg, wu, wd, taps_g, taps_u, prev_g, prev_u):
    g = _dot(hb, wg)
    u = _dot(hb, wu)
    cg = taps_g[0] * prev_g[0] + taps_g[1] * prev_g[1] + taps_g[2] * g + taps_g[3]
    cu = taps_u[0] * prev_u[0] + taps_u[1] * prev_u[1] + taps_u[2] * u + taps_u[3]
    return g, u, _dot((_silu(cg) * cu).astype(BF16), wd)


def _prompt_out_body(x_ref, yab_ref, yc_ref, mod_ref, n2g_ref, mixg_ref, wout_ref, wup_ref, wd_ref,
                     taps_ref, fg_ref, o_ref, fst_ref, buf_scr, halo_scr, act_scr, *, final):
    t = pl.program_id(1)
    tm = x_ref.shape[1]
    ab_w = yab_ref.shape[2]
    n_chunks = wup_ref.shape[1] // (2 * MXU_N)

    @pl.when(t == 0)
    def _():
        halo_scr[...] = jnp.zeros_like(halo_scr)

    y_c = (_rms(yc_ref[0]) * mixg_ref[:, ab_w:]).astype(BF16)
    attn = _dot(yab_ref[0], wout_ref[0:ab_w, :]) + _dot(y_c, wout_ref[ab_w:, :])
    x1 = x_ref[0] + mod_ref[0, 2:3, :] * attn
    hb = _modulated_norm(x1, n2g_ref[...], mod_ref[0, 4:5, :], mod_ref[0, 3:4, :]).astype(BF16)

    for c in range(n_chunks):
        conv = []
        for gu in range(2):
            slot = gu * n_chunks + c
            buf = buf_scr.at[c % 2, gu]
            cur = _dot(hb, wup_ref[:, slot * MXU_N:(slot + 1) * MXU_N])
            buf[0:SUBLANES, :] = halo_scr[slot]
            buf[SUBLANES:SUBLANES + tm, :] = cur
            tp = taps_ref[slot]
            conv.append(tp[0:1] * buf[pl.ds(SUBLANES - 2, tm), :] + tp[1:2] * buf[pl.ds(SUBLANES - 1, tm), :]
                        + tp[2:3] * cur + tp[3:4])
            tail = buf[tm:tm + SUBLANES, :]
            halo_scr[slot] = tail
            fst_ref[0, slot] = tail
        act_scr[:, c * MXU_N:(c + 1) * MXU_N] = (_silu(conv[0]) * conv[1]).astype(BF16)

    x2 = x1 + mod_ref[0, 5:6, :] * _dot(act_scr[...], wd_ref[...])
    o_ref[0] = _rms(x2) * fg_ref[...] if final else x2


def _resident(shape):
    return pl.BlockSpec(shape, lambda *_: (0,) * len(shape), pipeline_mode=pl.Buffered(1))


def _layer_spec(arr, layer, resident=False):
    mode = dict(pipeline_mode=pl.Buffered(1)) if resident else {}
    return pl.BlockSpec((None,) + arr.shape[1:], lambda *_: (layer,) + (0,) * (arr.ndim - 1), **mode)


def _prompt_out(x, yab, yc, mod, p, final_g, tm, final):
    nb, t, d = x.shape
    n_chunks = p["w_up"].shape[2] // (2 * MXU_N)
    body = functools.partial(_prompt_out_body, final=final)
    const = lambda *shape: pl.BlockSpec(shape, lambda b, i: (0,) * len(shape))
    return pl.pallas_call(
        body,
        grid=(nb, t // tm),
        in_specs=[pl.BlockSpec((1, tm, d), lambda b, i: (b, i, 0)),
                  pl.BlockSpec((1, tm, yab.shape[2]), lambda b, i: (b, i, 0)),
                  pl.BlockSpec((1, tm, yc.shape[2]), lambda b, i: (b, i, 0)),
                  pl.BlockSpec((1, 6, d), lambda b, i: (b, 0, 0)),
                  const(1, d), const(1, d),
                  _layer_spec(p["w_out"], p["layer"], True), _layer_spec(p["w_up"], p["layer"], True),
                  _layer_spec(p["w_down_full"], p["layer"], True), _resident(p["ffn_taps"].shape), const(1, d)],
        out_specs=[pl.BlockSpec((1, tm, d), lambda b, i: (b, i, 0)),
                   pl.BlockSpec((1, 2 * n_chunks, SUBLANES, MXU_N), lambda b, i: (b, 0, 0, 0))],
        out_shape=[jax.ShapeDtypeStruct((nb, t, d), F32),
                   jax.ShapeDtypeStruct((nb, 2 * n_chunks, SUBLANES, MXU_N), F32)],
        scratch_shapes=[pltpu.VMEM((2, 2, SUBLANES + tm, MXU_N), F32),
                        pltpu.VMEM((2 * n_chunks, SUBLANES, MXU_N), F32),
                        pltpu.VMEM((tm, n_chunks * MXU_N), BF16)],
        compiler_params=pltpu.CompilerParams(dimension_semantics=("arbitrary", "arbitrary"),
                                             vmem_limit_bytes=VMEM_LIMIT),
        name="prompt_out",
    )(x, yab, yc, mod, p["norm2_g"], p["mix_g"], p["w_out"], p["w_up"], p["w_down_full"],
      p["ffn_taps"], final_g)


def _sample_in_body(x_ref, mod_ref, n1g_ref, win_ref, wf_ref, bf_ref, alng_ref, alnb_ref, ws0_ref, bs0_ref,
                    cw_ref, cb_ref, clng_ref, clnb_ref, mixg_ref, st_ref,
                    q_ref, k_ref, v_ref, logf_ref, yab_ref, cv_ref, nst_ref, *, a_w, b_w, c_w, conv_w):
    d = x_ref.shape[1]
    n_heads = c_w // HEAD_DIM
    h = _modulated_norm(x_ref[...], n1g_ref[...], mod_ref[:, d:2 * d], mod_ref[:, 0:d])
    hb = h.astype(BF16)
    z = _dot(hb, win_ref[...])
    u = _gelu(z[:, 0:a_w])
    vln = _layernorm(_gelu(z[:, a_w:2 * a_w]), alng_ref[...], alnb_ref[...])
    cv_ref[...] = vln
    y_a = _rms(u * (ws0_ref[...] * vln + bs0_ref[...])) * mixg_ref[:, 0:a_w]
    yab_ref[:, 0:a_w] = y_a.astype(BF16)
    o = 2 * a_w
    glu = z[:, o:o + b_w] * _sigmoid(z[:, o + b_w:o + 2 * b_w])
    hist = (conv_w - 1) * b_w
    acc = cb_ref[...] + cw_ref[conv_w - 1:conv_w, :] * glu
    for j in range(conv_w - 1):
        acc = acc + cw_ref[j:j + 1, :] * st_ref[:, j * b_w:(j + 1) * b_w]
    y_b = _rms(_silu(_layernorm(acc, clng_ref[...], clnb_ref[...]))) * mixg_ref[:, a_w:a_w + b_w]
    yab_ref[:, a_w:a_w + b_w] = y_b.astype(BF16)
    nst_ref[:, 0:hist - b_w] = st_ref[:, b_w:hist]
    nst_ref[:, hist - b_w:hist] = glu
    o += 2 * b_w
    q_ref[...] = z[:, o:o + c_w]
    k_ref[...] = z[:, o + c_w:o + 2 * c_w]
    v_ref[...] = z[:, o + 2 * c_w:o + 3 * c_w]
    logf = _log_sigmoid(_dot(hb, wf_ref[...]) + bf_ref[...])
    logf_ref[...] = logf[:, 0:n_heads]


def _sample_in(x, mod, state, p):
    n, d = x.shape
    a_w, b_w, c_w = p["a_w"], p["b_w"], p["c_w"]
    n_heads = c_w // HEAD_DIM
    conv_w = p["conv_w"].shape[0]
    body = functools.partial(_sample_in_body, a_w=a_w, b_w=b_w, c_w=c_w, conv_w=conv_w)
    outs = [((n, c_w), F32), ((n, c_w), F32), ((n, c_w), F32), ((n, n_heads), F32),
            ((n, a_w + b_w), BF16), ((n, a_w), F32), (state.shape, F32)]
    whole = lambda shape: pl.BlockSpec(shape, lambda i: (0,) * len(shape))
    args = (x, mod, p["norm1_g"], p["w_in"], p["w_f"], p["b_f"], p["a_ln_g"], p["a_ln_b"], p["w_s0"], p["b_s0"],
            p["conv_w"], p["conv_b"], p["conv_ln_g"], p["conv_ln_b"], p["mix_g"], state)
    return pl.pallas_call(
        body,
        grid=(1,),
        in_specs=[_layer_spec(a, p["layer"]) if a is p["w_in"] else whole(a.shape) for a in args],
        out_specs=[whole(s) for s, _ in outs],
        out_shape=[jax.ShapeDtypeStruct(s, dt) for s, dt in outs],
        compiler_params=pltpu.CompilerParams(dimension_semantics=("arbitrary",), vmem_limit_bytes=VMEM_LIMIT),
        name="sample_in",
    )(*args)


def _sample_attn_body(pt_ref, q_ref, kn_ref, vn_ref, fn_ref, *refs, n_pages):
    k_refs, v_refs, f_ref = refs[:n_pages], refs[n_pages:2 * n_pages], refs[2 * n_pages]
    o_ref, qb_scr, m_scr, l_scr, c_scr, s_scr, p_scr, acc_scr = refs[2 * n_pages + 1:]
    gi = pl.program_id(1)
    n_heads = qb_scr.shape[0]
    first_page = (pl.num_programs(1) - 1 - gi) * n_pages

    def pair_column(row_ref, pair):
        return _row_to_col(row_ref[0][:, pair * LANES:(pair + 1) * LANES])

    @pl.when(gi == 0)
    def _():
        for pair in range(n_heads // 2):
            q_col = pair_column(q_ref, pair) * SCALE
            for e in range(2):
                qb_scr[2 * pair + e] = jnp.broadcast_to(q_col[e * HEAD_DIM:(e + 1) * HEAD_DIM], (HEAD_DIM, PAGE))
        m_scr[...] = jnp.full_like(m_scr, NEG)
        l_scr[...] = jnp.zeros_like(l_scr)
        acc_scr[...] = jnp.zeros_like(acc_scr)
        c_scr[...] = _row_to_col(fn_ref[0])

    r = lax.broadcasted_iota(jnp.int32, (PAGE, PAGE), 0)
    c = lax.broadcasted_iota(jnp.int32, (PAGE, PAGE), 1)
    later = jnp.where(r > c, 1.0, 0.0).astype(BF16)
    logf = jnp.concatenate([f_ref[pt_ref[pl.program_id(0), first_page + i]] for i in range(n_pages)], axis=0)
    within = _dot3_r(logf, later)
    page_total = within[:, 0:1] + logf[:, 0:1]
    carry = c_scr[...]
    after = [None] * n_pages
    for i in reversed(range(n_pages)):
        after[i] = carry
        carry = carry + page_total[i * n_heads:(i + 1) * n_heads]
        for h in range(n_heads):
            s_scr[pl.ds(i * n_heads + h, 1), :] = jnp.sum(k_refs[i][0, 0, h] * qb_scr[h], axis=0, keepdims=True)
    c_scr[...] = carry
    scores = s_scr[...] + within + jnp.concatenate(after, axis=0)

    m_old = m_scr[...]
    row_max = jnp.max(scores, axis=1, keepdims=True)
    m_new = m_old
    for i in range(n_pages):
        m_new = jnp.maximum(m_new, row_max[i * n_heads:(i + 1) * n_heads])
    alpha = jnp.exp(m_old - m_new)
    pr = jnp.exp(scores - jnp.concatenate([m_new] * n_pages, axis=0))
    p_scr[...] = pr
    row_sum = jnp.sum(pr, axis=1, keepdims=True)
    l = alpha * l_scr[...]
    for i in range(n_pages):
        l = l + row_sum[i * n_heads:(i + 1) * n_heads]
    m_scr[...] = m_new
    l_scr[...] = l
    for h in range(n_heads):
        acc = alpha[h:h + 1, :] * acc_scr[h]
        for i in range(n_pages):
            acc = acc + p_scr[pl.ds(i * n_heads + h, 1), :] * v_refs[i][0, 0, h]
        acc_scr[h] = acc

    @pl.when(gi == pl.num_programs(1) - 1)
    def _():
        for pair in range(n_heads // 2):
            q_col, k_col, v_col = (pair_column(r, pair) for r in (q_ref, kn_ref, vn_ref))
            outs = []
            for e in range(2):
                h = 2 * pair + e
                own = slice(e * HEAD_DIM, (e + 1) * HEAD_DIM)
                s_new = jnp.sum(q_col[own] * SCALE * k_col[own], axis=0, keepdims=True)
                m_h = m_new[h:h + 1, :]
                m_f = jnp.maximum(m_h, s_new)
                a_f = jnp.exp(m_h - m_f)
                p_new = jnp.exp(s_new - m_f)
                l_f = a_f * l[h:h + 1, :] + p_new
                past = jnp.sum(acc_scr[h], axis=1, keepdims=True)
                outs.append((a_f * past + p_new * v_col[own]) / l_f)
            o_ref[0, :, pair * LANES:(pair + 1) * LANES] = _col_to_row(jnp.concatenate(outs, axis=0))


def _sample_attn(q, k_new, v_new, logf_new, cache_k, cache_v, cache_logf, page_table, layer):
    n, width = q.shape
    n_heads = width // HEAD_DIM
    n_seq_pages = page_table.shape[1]
    pps = min(PAGES_PER_STEP, n_seq_pages)
    n_steps = n_seq_pages // pps
    kt = cache_k.transpose(0, 1, 3, 4, 2)
    vt = cache_v.transpose(0, 1, 3, 4, 2)
    ft = cache_logf.transpose(0, 1, 3, 2)

    def page(i, *tail):
        return pl.BlockSpec((1, 1) + tail,
                            lambda b, g, pt: (layer, pt[b, (n_steps - 1 - g) * pps + i]) + (0,) * len(tail))

    row = lambda last: pl.BlockSpec((1, 1, last), lambda b, g, pt: (b, 0, 0))
    grid_spec = pltpu.PrefetchScalarGridSpec(
        num_scalar_prefetch=1,
        grid=(n, n_steps),
        in_specs=[row(width), row(width), row(width), row(n_heads)]
                 + [page(i, n_heads, HEAD_DIM, PAGE) for i in range(pps)] * 2
                 + [pl.BlockSpec((None,) + ft.shape[1:], lambda b, g, pt: (layer, 0, 0, 0),
                                 pipeline_mode=pl.Buffered(1))],
        out_specs=row(width),
        scratch_shapes=[pltpu.VMEM((n_heads, HEAD_DIM, PAGE), F32),
                        pltpu.VMEM((n_heads, 1), F32), pltpu.VMEM((n_heads, 1), F32), pltpu.VMEM((n_heads, 1), F32),
                        pltpu.VMEM((pps * n_heads, PAGE), F32), pltpu.VMEM((pps * n_heads, PAGE), F32),
                        pltpu.VMEM((n_heads, HEAD_DIM, PAGE), F32)],
    )
    as_rows = lambda a: a.reshape(n, 1, a.shape[1])
    out = pl.pallas_call(
        functools.partial(_sample_attn_body, n_pages=pps),
        grid_spec=grid_spec,
        out_shape=jax.ShapeDtypeStruct((n, 1, width), F32),
        compiler_params=pltpu.CompilerParams(dimension_semantics=("arbitrary", "arbitrary"),
                                             vmem_limit_bytes=VMEM_LIMIT),
        name="sample_attn",
    )(page_table, as_rows(q), as_rows(k_new), as_rows(v_new), as_rows(logf_new),
      *([kt] * pps), *([vt] * pps), ft)
    return out.reshape(n, width)


def _sample_out_body(x_ref, yab_ref, yc_ref, mod_ref, n2g_ref, mixg_ref, wout_ref, wg_ref, wu_ref, wd_ref,
                     taps_g_ref, taps_u_ref, s0g_ref, s0u_ref, s1g_ref, s1u_ref, fg_ref,
                     o_ref, ng_ref, nu_ref, pg_ref, pu_ref, x1_scr, hb_scr, acc_scr, *, final):
    c = pl.program_id(0)
    d = x_ref.shape[1]
    ab_w = yab_ref.shape[1]

    @pl.when(c == 0)
    def _():
        y_c = (_rms(yc_ref[...]) * mixg_ref[:, ab_w:]).astype(BF16)
        attn = _dot(yab_ref[...], wout_ref[0:ab_w, :]) + _dot(y_c, wout_ref[ab_w:, :])
        x1 = x_ref[...] + mod_ref[:, 2 * d:3 * d] * attn
        x1_scr[...] = x1
        hb_scr[...] = _modulated_norm(x1, n2g_ref[...], mod_ref[:, 4 * d:5 * d], mod_ref[:, 3 * d:4 * d]).astype(BF16)
        acc_scr[...] = jnp.zeros_like(acc_scr)

    tg, tu = taps_g_ref[0], taps_u_ref[0]
    g, u, down = _ffn_chunk(hb_scr[...], wg_ref[...], wu_ref[...], wd_ref[...],
                            [tg[i:i + 1] for i in range(4)], [tu[i:i + 1] for i in range(4)],
                            (s0g_ref[...], s1g_ref[...]), (s0u_ref[...], s1u_ref[...]))
    ng_ref[...] = g
    nu_ref[...] = u
    pg_ref[...] = s1g_ref[...]
    pu_ref[...] = s1u_ref[...]
    acc_scr[...] += down

    @pl.when(c == pl.num_programs(0) - 1)
    def _():
        x2 = x1_scr[...] + mod_ref[:, 5 * d:6 * d] * acc_scr[...]
        o_ref[...] = _rms(x2) * fg_ref[...] if final else x2


def _sample_out(x, yab, yc, mod, state, p, final_g, final):
    n, d = x.shape
    n_chunks = p["w_up"].shape[2] // (2 * MXU_N)
    d_ff = n_chunks * MXU_N
    const = lambda *shape: pl.BlockSpec(shape, lambda c: (0,) * len(shape))
    st = lambda off: pl.BlockSpec((n, MXU_N), lambda c: (0, off + c))
    col = pl.BlockSpec((n, MXU_N), lambda c: (0, c))
    outs = pl.pallas_call(
        functools.partial(_sample_out_body, final=final),
        grid=(n_chunks,),
        in_specs=[const(n, d), const(*yab.shape), const(*yc.shape), const(*mod.shape), const(1, d), const(1, d),
                  _layer_spec(p["w_out"], p["layer"]),
                  pl.BlockSpec((None, d, MXU_N), lambda c: (p["layer"], 0, c)),
                  pl.BlockSpec((None, d, MXU_N), lambda c: (p["layer"], 0, n_chunks + c)),
                  pl.BlockSpec((None, MXU_N, d), lambda c: (p["layer"], c, 0)),
                  pl.BlockSpec((1, SUBLANES, MXU_N), lambda c: (c, 0, 0)),
                  pl.BlockSpec((1, SUBLANES, MXU_N), lambda c: (n_chunks + c, 0, 0)),
                  st(0), st(n_chunks), st(2 * n_chunks), st(3 * n_chunks), const(1, d)],
        out_specs=[const(n, d), col, col, col, col],
        out_shape=[jax.ShapeDtypeStruct((n, d), F32)] + [jax.ShapeDtypeStruct((n, d_ff), F32)] * 4,
        scratch_shapes=[pltpu.VMEM((n, d), F32), pltpu.VMEM((n, d), BF16), pltpu.VMEM((n, d), F32)],
        compiler_params=pltpu.CompilerParams(dimension_semantics=("arbitrary",), vmem_limit_bytes=VMEM_LIMIT),
        name="sample_out",
    )(x, yab, yc, mod, p["norm2_g"], p["mix_g"], p["w_out"], p["w_up"], p["w_up"], p["w_down_full"],
      p["ffn_taps"], p["ffn_taps"], state, state, state, state, final_g)
    x2, new_g, new_u, prev_g, prev_u = outs
    new_state = jnp.stack([jnp.concatenate([prev_g, prev_u], axis=1), jnp.concatenate([new_g, new_u], axis=1)], axis=1)
    return x2, new_state


def _mxu_weights(w_in, w_out, w_up, w_down, main, c_w):
    return dict(w_in=w_in[:, :, :main].astype(BF16),
                w_vt=w_in[:, :, main - c_w:main].transpose(0, 2, 1).astype(BF16),
                w_out=w_out.astype(BF16), w_up=w_up.astype(BF16), w_down_full=w_down.astype(BF16))


def _layer_params(l, d, mxu, norm1_g, norm2_g, w_in, b_forget, a_ln_g, a_ln_b, w_s, b_s, conv_w, conv_b,
                  conv_ln_g, conv_ln_b, mix_g, ffn_conv_w, ffn_conv_b):
    a_w, b_w, n_heads = a_ln_g.shape[1], conv_b.shape[1], b_forget.shape[1]
    c_w = n_heads * HEAD_DIM
    main = 2 * a_w + 2 * b_w + 3 * c_w
    d_ff = mxu["w_down_full"].shape[1]
    n_chunks = d_ff // MXU_N
    row = lambda v: v.reshape(1, -1)
    pad_lanes = lambda a: jnp.pad(a, ((0, 0), (0, LANES - a.shape[1])))
    decay_q, decay_k = _decay_placement(n_heads)
    chunked = lambda v: v.reshape(2 * n_chunks, MXU_N)
    taps = jnp.stack([chunked(ffn_conv_w[l, 0]), chunked(ffn_conv_w[l, 1]), chunked(ffn_conv_w[l, 2]),
                      chunked(ffn_conv_b[l])], axis=1)
    taps = jnp.pad(taps, ((0, 0), (0, SUBLANES - 4), (0, 0)))
    return dict(
        mxu, layer=l,
        a_w=a_w, b_w=b_w, c_w=c_w,
        norm1_g=row(norm1_g[l]), norm2_g=row(norm2_g[l]),
        w_f=pad_lanes(jnp.tile(w_in[l, :, main:], (1, 3))).astype(BF16),
        b_f=pad_lanes(jnp.tile(row(b_forget[l]), (1, 3))),
        decay_q=decay_q, decay_k=decay_k,
        a_ln_g=row(a_ln_g[l]), a_ln_b=row(a_ln_b[l]),
        w_s=w_s[l], b_s_full=jnp.repeat(b_s[l].T, HEAD_DIM, axis=1),
        w_s0=row(jnp.repeat(w_s[l, :, 0, 0], HEAD_DIM)), b_s0=row(jnp.repeat(b_s[l, :, 0], HEAD_DIM)),
        conv_w=conv_w[l], conv_b=row(conv_b[l]), conv_ln_g=row(conv_ln_g[l]), conv_ln_b=row(conv_ln_b[l]),
        mix_g=row(mix_g[l]),
        ffn_taps=taps,
    )


def kernel(x_prompt, x_sample, cache_k, cache_v, cache_logf, state_conv, state_ffn_conv, page_table, c_prompt, c_sample, norm1_g, norm2_g, w_ada, b_ada, w_in, b_forget, a_ln_g, a_ln_b, w_s, b_s, conv_w, conv_b, conv_ln_g, conv_ln_b, mix_g, w_out, w_up, ffn_conv_w, ffn_conv_b, w_down, final_g):
    nb, t, d = x_prompt.shape
    ns = x_sample.shape[0]
    depth = w_in.shape[0]
    n_heads = b_forget.shape[1]
    tm = min(ROW_TILE, t)

    c_all = jnp.concatenate([c_prompt, c_sample], axis=0)
    pad_rows = (-c_all.shape[0]) % 16
    mod = _ada(jnp.pad(c_all, ((0, pad_rows), (0, 0))), w_ada, b_ada)
    fg = final_g.reshape(1, d)

    c_w = n_heads * HEAD_DIM
    mxu = _mxu_weights(w_in, w_out, w_up, w_down, w_in.shape[2] - n_heads, c_w)
    xp, xs = x_prompt, x_sample.reshape(ns, d)
    prompt_out, sample_out = [], []
    for l in range(depth):
        p = _layer_params(l, d, mxu, norm1_g, norm2_g, w_in, b_forget, a_ln_g, a_ln_b, w_s, b_s, conv_w, conv_b,
                          conv_ln_g, conv_ln_b, mix_g, ffn_conv_w, ffn_conv_b)
        n_chunks = p["w_up"].shape[2] // (2 * MXU_N)
        final = l == depth - 1
        mod_p = mod[l, :nb].reshape(nb, 6, d)
        mod_s = mod[l, nb:nb + ns]

        k, vt, logf, qh, kh, vh, yab, conv_st = _prompt_in(xp, mod_p, p, tm)
        yc = _prompt_attn(qh, kh, vh, tm)
        xp, ffn_tail = _prompt_out(xp, yab, yc, mod_p, p, fg, tm, final)
        ffn_st = ffn_tail[:, :, SUBLANES - 2:, :].transpose(0, 2, 1, 3).reshape(nb, 2, 2 * n_chunks * MXU_N)
        v_rows = vt.reshape(nb, n_heads, HEAD_DIM, t).transpose(0, 3, 1, 2)
        prompt_out.append((k.reshape(nb, t, n_heads, HEAD_DIM), v_rows, logf, conv_st, ffn_st))

        st_conv = state_conv[l].reshape(ns, -1)
        q_s, k_s, v_s, logf_s, yab_s, chunk_v, new_conv = _sample_in(xs, mod_s, st_conv, p)
        yc_s = _sample_attn(q_s, k_s, v_s, logf_s, cache_k, cache_v, cache_logf, page_table, l)
        xs, new_ffn = _sample_out(xs, yab_s, yc_s, mod_s, state_ffn_conv[l].reshape(ns, -1), p, fg, final)
        sample_out.append((k_s.reshape(ns, 1, n_heads, HEAD_DIM), v_s.reshape(ns, 1, n_heads, HEAD_DIM),
                           logf_s.reshape(ns, 1, n_heads), new_conv.reshape(state_conv.shape[1:]), new_ffn,
                           chunk_v.reshape(ns, 1, -1)))

    stack = lambda outs, i: jnp.stack([o[i] for o in outs])
    return (xp, xs.reshape(ns, 1, d),
            stack(prompt_out, 0), stack(prompt_out, 1), stack(prompt_out, 2), stack(prompt_out, 3), stack(prompt_out, 4),
            stack(sample_out, 0), stack(sample_out, 1), stack(sample_out, 2), stack(sample_out, 3), stack(sample_out, 4),
            stack(sample_out, 5))
```

```python
import functools

import numpy as np

import jax
import jax.numpy as jnp
from jax import lax
from jax.experimental import pallas as pl
from jax.experimental.pallas import tpu as pltpu

F32 = jnp.float32
BF16 = jnp.bfloat16

HEAD_DIM = 64
CHUNK = 128
PAGE = 128
EPS = 1e-6
SCALE = HEAD_DIM ** -0.5
NEG = -1e30
LOG2E = 1.4426950408889634

LANES = 128
SUBLANES = 8
MXU_N = 256
VMEM_LIMIT = 56 * 1024 * 1024

ROW_TILE = 512
CONV_ROWS = 64
ATTN_ROWS = 32
HALO = 32
PAGES_PER_STEP = 32


def _dot(a, b):
    return jnp.dot(a, b, preferred_element_type=F32)


def _dot_nt(a, b):
    return lax.dot_general(a, b, (((1,), (1,)), ((), ())), preferred_element_type=F32)


def _rms(x):
    return x * lax.rsqrt(jnp.mean(x * x, axis=-1, keepdims=True) + EPS)


def _layernorm(x, g, b):
    xc = x - jnp.mean(x, axis=-1, keepdims=True)
    return xc * lax.rsqrt(jnp.mean(xc * xc, axis=-1, keepdims=True) + EPS) * g + b


def _gelu(x):
    return 0.5 * x * (1.0 + lax.erf(x * (2.0 ** -0.5)))


def _sigmoid(x):
    return 1.0 / (1.0 + jnp.exp(-x))


def _silu(x):
    return x * _sigmoid(x)


def _log_sigmoid(x):
    return jnp.minimum(x, 0.0) - jnp.log(1.0 + jnp.exp(-jnp.abs(x)))


def _split3(x):
    hi = x.astype(BF16)
    r = x - hi.astype(F32)
    mid = r.astype(BF16)
    lo = (r - mid.astype(F32)).astype(BF16)
    return hi, mid, lo


def _dot3(w01, x):
    hi, mid, lo = _split3(x)
    return _dot(w01, hi) + _dot(w01, mid) + _dot(w01, lo)


def _dot3_r(x, w01):
    hi, mid, lo = _split3(x)
    return _dot(hi, w01) + _dot(mid, w01) + _dot(lo, w01)


def _diagonal(vec, n):
    r = lax.broadcasted_iota(jnp.int32, (n, n), 0)
    c = lax.broadcasted_iota(jnp.int32, (n, n), 1)
    return jnp.where(r == c, jnp.broadcast_to(vec, (n, n)), 0.0)


def _row_to_col(row):
    return jnp.sum(_diagonal(row, row.shape[1]), axis=1, keepdims=True)


def _col_to_row(col):
    return jnp.sum(_diagonal(col, col.shape[0]), axis=0, keepdims=True)


def _modulated_norm(x, g, scale, shift):
    return _rms(x) * g * (1.0 + scale) + shift


def _ada_body(c_ref, w_ref, b_ref, o_ref):
    c = c_ref[...]
    o_ref[0] = _dot(_silu(c).astype(BF16), w_ref[0].astype(BF16)) + b_ref[0]


def _ada(c_all, w_ada, b_ada):
    depth, d, n = w_ada.shape
    rows = c_all.shape[0]
    tn = n // 4
    return pl.pallas_call(
        _ada_body,
        grid=(depth, n // tn),
        in_specs=[pl.BlockSpec((rows, d), lambda l, j: (0, 0)),
                  pl.BlockSpec((1, d, tn), lambda l, j: (l, 0, j)),
                  pl.BlockSpec((1, 1, tn), lambda l, j: (l, 0, j))],
        out_specs=pl.BlockSpec((1, rows, tn), lambda l, j: (l, 0, j)),
        out_shape=jax.ShapeDtypeStruct((depth, rows, n), F32),
        compiler_params=pltpu.CompilerParams(dimension_semantics=("arbitrary", "arbitrary"),
                                             vmem_limit_bytes=VMEM_LIMIT),
        name="ada_mod",
    )(c_all, w_ada, b_ada.reshape(depth, 1, n))


def _own_half(rows, h):
    lane = lax.broadcasted_iota(jnp.int32, (rows, LANES), 1)
    parity = h % 2
    return lane, (lane >= HEAD_DIM * parity) & (lane < HEAD_DIM * (parity + 1))


def _decay_placement(n_heads):
    pq = np.zeros((LANES, n_heads * LANES), np.float32)
    pk = np.zeros((LANES, n_heads * LANES), np.float32)
    one_lane = 3 * n_heads
    for h in range(n_heads):
        base = h * LANES + HEAD_DIM * (1 - h % 2)
        for i in range(3):
            pq[i * n_heads + h, base + i] = 1.0
            pq[one_lane, base + 3 + i] = 1.0
            pk[one_lane, base + i] = 1.0
            pk[i * n_heads + h, base + 3 + i] = -1.0
    return jnp.asarray(pq, BF16), jnp.asarray(pk, BF16)


def _prompt_in_body(x_ref, mod_ref, n1g_ref, win_ref, wvt_ref, wf_ref, bf_ref, alng_ref, alnb_ref, ws_ref, bsf_ref,
                    cw_ref, cb_ref, clng_ref, clnb_ref, mixg_ref, tri_ref, pq_ref, pk_ref,
                    k_ref, vt_ref, logf_ref, qh_ref, kh_ref, vh_ref, yab_ref, cst_ref,
                    xp_scr, xs_scr, carry_scr, *, a_w, b_w, c_w, conv_w):
    t = pl.program_id(1)
    tm = x_ref.shape[1]
    n_heads = c_w // HEAD_DIM

    @pl.when(t == 0)
    def _():
        xp_scr[0:HALO, :] = jnp.zeros((HALO, b_w), F32)
        carry_scr[...] = jnp.zeros_like(carry_scr)

    x = x_ref[0]
    h = _modulated_norm(x, n1g_ref[...], mod_ref[0, 1:2, :], mod_ref[0, 0:1, :])
    hb = h.astype(BF16)

    z_uv = _dot(hb, win_ref[:, 0:2 * a_w])
    u = _gelu(z_uv[:, :a_w])
    vln = _layernorm(_gelu(z_uv[:, a_w:]), alng_ref[...], alnb_ref[...]).astype(BF16)
    row = lax.broadcasted_iota(jnp.int32, (CHUNK, CHUNK), 0)
    col = lax.broadcasted_iota(jnp.int32, (CHUNK, CHUNK), 1)
    lane_head = lax.broadcasted_iota(jnp.int32, (CHUNK, a_w), 1) // HEAD_DIM
    w_tril = [jnp.where(col <= row, ws_ref[hh], 0.0).astype(BF16) for hh in range(a_w // HEAD_DIM)]
    sv_chunks = []
    for c in range(tm // CHUNK):
        vc = vln[c * CHUNK:(c + 1) * CHUNK]
        sv = bsf_ref[...]
        for hh, w in enumerate(w_tril):
            sv = sv + jnp.where(lane_head == hh, _dot(w, vc), 0.0)
        sv_chunks.append(sv)
    y_a = _rms(u * jnp.concatenate(sv_chunks, axis=0)) * mixg_ref[:, 0:a_w]
    yab_ref[0, :, 0:a_w] = y_a.astype(BF16)

    z_ag = _dot(hb, win_ref[:, 2 * a_w:2 * a_w + 2 * b_w])
    xp_scr[HALO:HALO + tm, :] = z_ag[:, :b_w] * _sigmoid(z_ag[:, b_w:])
    shifted_rows = xs_scr.shape[1]
    for r in range(1, SUBLANES):
        xs_scr[r - 1] = xp_scr[pl.ds(r, shifted_rows), :]
    first_tap = HALO - (conv_w - 1)
    conv_blocks = []
    for rb in range(tm // CONV_ROWS):
        acc = jnp.broadcast_to(cb_ref[...], (CONV_ROWS, b_w))
        for j in range(conv_w):
            whole, r = divmod(first_tap + j, SUBLANES)
            src = xp_scr if r == 0 else xs_scr.at[r - 1]
            acc = acc + cw_ref[j:j + 1, :] * src[pl.ds(rb * CONV_ROWS + whole * SUBLANES, CONV_ROWS), :]
        conv_blocks.append(acc)
    conv = jnp.concatenate(conv_blocks, axis=0)
    y_b = _rms(_silu(_layernorm(conv, clng_ref[...], clnb_ref[...]))) * mixg_ref[:, a_w:a_w + b_w]
    yab_ref[0, :, a_w:a_w + b_w] = y_b.astype(BF16)
    cst_ref[0] = xp_scr[pl.ds(HALO + tm - (conv_w - 1), conv_w - 1), :]
    xp_scr[0:HALO, :] = xp_scr[tm:tm + HALO, :]

    o = 2 * a_w + 2 * b_w
    z_q = _dot(hb, win_ref[:, o:o + c_w]) * (SCALE * LOG2E)
    z_k = _dot(hb, win_ref[:, o + c_w:o + 2 * c_w])
    z_vt = _dot_nt(wvt_ref[...], hb)
    k_ref[0] = z_k
    vt_ref[0] = z_vt
    logf = _log_sigmoid(_dot(hb, wf_ref[...]) + bf_ref[...])
    logf_ref[0] = logf[:, 0:n_heads]
    cum = _dot3(tri_ref[...], logf) + carry_scr[...]
    carry_scr[...] = cum[tm - 1:tm, :]
    hi, mid, lo = _split3(cum * LOG2E)
    lane = lax.broadcasted_iota(jnp.int32, (tm, LANES), 1)
    one = jnp.where(lane == 3 * n_heads, 1.0, 0.0).astype(BF16)
    packed = jnp.where(lane < n_heads, hi, jnp.where(lane < 2 * n_heads, mid, jnp.where(lane < 3 * n_heads, lo, one)))
    decay_q = _dot(packed, pq_ref[...])
    decay_k = _dot(packed, pk_ref[...])
    for hh in range(n_heads):
        slab = slice((hh // 2) * LANES, (hh // 2 + 1) * LANES)
        mine = slice(hh * LANES, (hh + 1) * LANES)
        own = _own_half(tm, hh)[1]
        qh_ref[0, hh] = jnp.where(own, z_q[:, slab], decay_q[:, mine]).astype(BF16)
        kh_ref[0, hh] = jnp.where(own, z_k[:, slab], decay_k[:, mine]).astype(BF16)
        feat = lax.broadcasted_iota(jnp.int32, (LANES, tm), 0) // HEAD_DIM
        vh_ref[0, hh, 0] = jnp.where(feat == hh % 2, z_vt[slab, :], 1.0).astype(BF16)


def _prompt_in(x, mod, p, tm):
    nb, t, d = x.shape
    a_w, b_w, c_w = p["a_w"], p["b_w"], p["c_w"]
    n_heads = c_w // HEAD_DIM
    conv_w = p["conv_w"].shape[0]
    const = lambda *shape: pl.BlockSpec(shape, lambda b, i: (0,) * len(shape))
    body = functools.partial(_prompt_in_body, a_w=a_w, b_w=b_w, c_w=c_w, conv_w=conv_w)
    return pl.pallas_call(
        body,
        grid=(nb, t // tm),
        in_specs=[pl.BlockSpec((1, tm, d), lambda b, i: (b, i, 0)),
                  pl.BlockSpec((1, 6, d), lambda b, i: (b, 0, 0)),
                  const(1, d), _layer_spec(p["w_in"], p["layer"]), _layer_spec(p["w_vt"], p["layer"]),
                  const(d, LANES), const(1, LANES),
                  const(1, a_w), const(1, a_w), const(*p["w_s"].shape), const(CHUNK, a_w),
                  const(*p["conv_w"].shape), const(1, b_w), const(1, b_w), const(1, b_w), const(1, d),
                  const(tm, tm), const(*p["decay_q"].shape), const(*p["decay_k"].shape)],
        out_specs=[pl.BlockSpec((1, tm, c_w), lambda b, i: (b, i, 0)),
                   pl.BlockSpec((1, c_w, tm), lambda b, i: (b, 0, i)),
                   pl.BlockSpec((1, tm, n_heads), lambda b, i: (b, i, 0)),
                   pl.BlockSpec((1, n_heads, tm, LANES), lambda b, i: (b, 0, i, 0)),
                   pl.BlockSpec((1, n_heads, tm, LANES), lambda b, i: (b, 0, i, 0)),
                   pl.BlockSpec((1, n_heads, 1, LANES, tm), lambda b, i: (b, 0, i, 0, 0)),
                   pl.BlockSpec((1, tm, a_w + b_w), lambda b, i: (b, i, 0)),
                   pl.BlockSpec((1, conv_w - 1, b_w), lambda b, i: (b, 0, 0))],
        out_shape=[jax.ShapeDtypeStruct((nb, t, c_w), F32),
                   jax.ShapeDtypeStruct((nb, c_w, t), F32),
                   jax.ShapeDtypeStruct((nb, t, n_heads), F32),
                   jax.ShapeDtypeStruct((nb, n_heads, t, LANES), BF16),
                   jax.ShapeDtypeStruct((nb, n_heads, t, LANES), BF16),
                   jax.ShapeDtypeStruct((nb, n_heads, t // tm, LANES, tm), BF16),
                   jax.ShapeDtypeStruct((nb, t, a_w + b_w), BF16),
                   jax.ShapeDtypeStruct((nb, conv_w - 1, b_w), F32)],
        scratch_shapes=[pltpu.VMEM((HALO + tm, b_w), F32),
                        pltpu.VMEM((SUBLANES - 1, HALO + tm - SUBLANES, b_w), F32),
                        pltpu.VMEM((1, LANES), F32)],
        compiler_params=pltpu.CompilerParams(dimension_semantics=("arbitrary", "arbitrary"),
                                             vmem_limit_bytes=VMEM_LIMIT),
        name="prompt_in",
    )(x, mod, p["norm1_g"], p["w_in"], p["w_vt"], p["w_f"], p["b_f"], p["a_ln_g"], p["a_ln_b"], p["w_s"],
      p["b_s_full"], p["conv_w"], p["conv_b"], p["conv_ln_g"], p["conv_ln_b"], p["mix_g"],
      jnp.tril(jnp.ones((tm, tm), BF16)), p["decay_q"], p["decay_k"])


def _softmax_keys_on_rows(s_ref, p_ref, m_ref, a_ref, first_key):
    tk, tq = s_ref.shape

    def visible(r0, c0):
        if first_key is None or first_key + r0 + ATTN_ROWS - 1 <= c0:
            return "all"
        return "none" if first_key + r0 > c0 + LANES - 1 else "some"

    def chunk(r0, c0):
        sc = s_ref[r0:r0 + ATTN_ROWS, c0:c0 + LANES]
        if visible(r0, c0) == "some":
            key = first_key + r0 + lax.broadcasted_iota(jnp.int32, (ATTN_ROWS, LANES), 0)
            qry = c0 + lax.broadcasted_iota(jnp.int32, (ATTN_ROWS, LANES), 1)
            sc = jnp.where(key <= qry, sc, NEG)
        return sc

    for c0 in range(0, tq, LANES):
        cols = slice(c0, c0 + LANES)
        seen = [r0 for r0 in range(0, tk, ATTN_ROWS) if visible(r0, c0) != "none"]
        if not seen:
            a_ref[:, cols] = jnp.ones((1, LANES), F32)
            p_ref[:, cols] = jnp.zeros((tk, LANES), BF16)
            continue
        top = chunk(seen[0], c0)
        for r0 in seen[1:]:
            top = jnp.maximum(top, chunk(r0, c0))
        m_old = m_ref[:, cols]
        m_new = jnp.maximum(m_old, jnp.max(top, axis=0, keepdims=True))
        m_ref[:, cols] = m_new
        a_ref[:, cols] = jnp.exp2(m_old - m_new)
        for r0 in range(0, tk, ATTN_ROWS):
            if r0 in seen:
                p_ref[r0:r0 + ATTN_ROWS, cols] = jnp.exp2(chunk(r0, c0) - m_new).astype(BF16)
            else:
                p_ref[r0:r0 + ATTN_ROWS, cols] = jnp.zeros((ATTN_ROWS, LANES), BF16)


def _prompt_attn_body(q_ref, k_ref, v_ref, o_ref, s_scr, p_scr, m_scr, a_scr, acc_scr):
    i = pl.program_id(2)
    tq = q_ref.shape[2]
    m_scr[...] = jnp.full_like(m_scr, NEG)
    acc_scr[...] = jnp.zeros_like(acc_scr)

    tk = tq // 2

    def scores(j, half):
        start = pl.multiple_of(j * tq + half * tk, tk)
        for e in range(2):
            s_scr[half, e] = _dot_nt(k_ref[0, e, pl.ds(start, tk), :], q_ref[0, e])

    def consume(j, half, masked):
        for e in range(2):
            _softmax_keys_on_rows(s_scr.at[half, e], p_scr.at[half, e], m_scr.at[e], a_scr.at[e],
                                  half * tk if masked else None)
            acc_scr[e] = a_scr[e] * acc_scr[e] + _dot(v_ref[0, e, j, :, half * tk:(half + 1) * tk], p_scr[half, e])

    def past_block(j):
        scores(j, 1)
        consume(j, 0, False)
        scores(j + 1, 0)
        consume(j, 1, False)

    def past_pair(jj, _):
        past_block(2 * jj)
        past_block(2 * jj + 1)
        return 0

    scores(0, 0)
    lax.fori_loop(0, i // 2, past_pair, 0)

    @pl.when(i % 2 == 1)
    def _():
        past_block(i - 1)

    scores(i, 1)
    consume(i, 0, True)
    consume(i, 1, True)
    halves = []
    for e in range(2):
        own, other = slice(e * HEAD_DIM, (e + 1) * HEAD_DIM), slice((1 - e) * HEAD_DIM, (2 - e) * HEAD_DIM)
        halves.append(acc_scr[e, own, :] / acc_scr[e, other, :])
    o_ref[0] = jnp.concatenate(halves, axis=0).T


def _prompt_attn(qh, kh, vh, tq):
    nb, n_heads, t, _ = qh.shape
    pair = lambda rows, imap: pl.BlockSpec((1, 2, rows, LANES), imap)
    return pl.pallas_call(
        _prompt_attn_body,
        grid=(nb, n_heads // 2, t // tq),
        in_specs=[pair(tq, lambda b, j, i: (b, j, i, 0)),
                  pair(t, lambda b, j, i: (b, j, 0, 0)),
                  pl.BlockSpec((1, 2, t // tq, LANES, tq), lambda b, j, i: (b, j, 0, 0, 0))],
        out_specs=pl.BlockSpec((1, tq, LANES), lambda b, j, i: (b, i, j)),
        out_shape=jax.ShapeDtypeStruct((nb, t, n_heads * HEAD_DIM), F32),
        scratch_shapes=[pltpu.VMEM((2, 2, tq // 2, tq), F32), pltpu.VMEM((2, 2, tq // 2, tq), BF16),
                        pltpu.VMEM((2, 1, tq), F32), pltpu.VMEM((2, 1, tq), F32), pltpu.VMEM((2, LANES, tq), F32)],
        compiler_params=pltpu.CompilerParams(dimension_semantics=("arbitrary", "arbitrary", "arbitrary"),
                                             vmem_limit_bytes=VMEM_LIMIT),
        name="prompt_attn",
    )(qh, kh, vh)


def _ffn_chunk(hb, wg, wu, wd, taps_g, taps_u, prev_g, prev_u):
    g = _dot(hb, wg)
    u = _dot(hb, wu)
    cg = taps_g[0] * prev_g[0] + taps_g[1] * prev_g[1] + taps_g[2] * g + taps_g[3]
    cu = taps_u[0] * prev_u[0] + taps_u[1] * prev_u[1] + taps_u[2] * u + taps_u[3]
    return g, u, _dot((_silu(cg) * cu).astype(BF16), wd)


def _prompt_out_body(x_ref, yab_ref, yc_ref, mod_ref, n2g_ref, mixg_ref, wout_ref, wup_ref, wd_ref,
                     taps_ref, fg_ref, o_ref, fst_ref, buf_scr, halo_scr, act_scr, *, final):
    t = pl.program_id(1)
    tm = x_ref.shape[1]
    ab_w = yab_ref.shape[2]
    n_chunks = wup_ref.shape[1] // (2 * MXU_N)

    @pl.when(t == 0)
    def _():
        halo_scr[...] = jnp.zeros_like(halo_scr)

    y_c = (_rms(yc_ref[0]) * mixg_ref[:, ab_w:]).astype(BF16)
    attn = _dot(yab_ref[0], wout_ref[0:ab_w, :]) + _dot(y_c, wout_ref[ab_w:, :])
    x1 = x_ref[0] + mod_ref[0, 2:3, :] * attn
    hb = _modulated_norm(x1, n2g_ref[...], mod_ref[0, 4:5, :], mod_ref[0, 3:4, :]).astype(BF16)

    for c in range(n_chunks):
        conv = []
        for gu in range(2):
            slot = gu * n_chunks + c
            buf = buf_scr.at[c % 2, gu]
            cur = _dot(hb, wup_ref[:, slot * MXU_N:(slot + 1) * MXU_N])
            buf[0:SUBLANES, :] = halo_scr[slot]
            buf[SUBLANES:SUBLANES + tm, :] = cur
            tp = taps_ref[slot]
            conv.append(tp[0:1] * buf[pl.ds(SUBLANES - 2, tm), :] + tp[1:2] * buf[pl.ds(SUBLANES - 1, tm), :]
                        + tp[2:3] * cur + tp[3:4])
            tail = buf[tm:tm + SUBLANES, :]
            halo_scr[slot] = tail
            fst_ref[0, slot] = tail
        act_scr[:, c * MXU_N:(c + 1) * MXU_N] = (_silu(conv[0]) * conv[1]).astype(BF16)

    x2 = x1 + mod_ref[0, 5:6, :] * _dot(act_scr[...], wd_ref[...])
    o_ref[0] = _rms(x2) * fg_ref[...] if final else x2


def _resident(shape):
    return pl.BlockSpec(shape, lambda *_: (0,) * len(shape), pipeline_mode=pl.Buffered(1))


def _layer_spec(arr, layer, resident=False):
    mode = dict(pipeline_mode=pl.Buffered(1)) if resident else {}
    return pl.BlockSpec((None,) + arr.shape[1:], lambda *_: (layer,) + (0,) * (arr.ndim - 1), **mode)


def _prompt_out(x, yab, yc, mod, p, final_g, tm, final):
    nb, t, d = x.shape
    n_chunks = p["w_up"].shape[2] // (2 * MXU_N)
    body = functools.partial(_prompt_out_body, final=final)
    const = lambda *shape: pl.BlockSpec(shape, lambda b, i: (0,) * len(shape))
    return pl.pallas_call(
        body,
        grid=(nb, t // tm),
        in_specs=[pl.BlockSpec((1, tm, d), lambda b, i: (b, i, 0)),
                  pl.BlockSpec((1, tm, yab.shape[2]), lambda b, i: (b, i, 0)),
                  pl.BlockSpec((1, tm, yc.shape[2]), lambda b, i: (b, i, 0)),
                  pl.BlockSpec((1, 6, d), lambda b, i: (b, 0, 0)),
                  const(1, d), const(1, d),
                  _layer_spec(p["w_out"], p["layer"], True), _layer_spec(p["w_up"], p["layer"], True),
                  _layer_spec(p["w_down_full"], p["layer"], True), _resident(p["ffn_taps"].shape), const(1, d)],
        out_specs=[pl.BlockSpec((1, tm, d), lambda b, i: (b, i, 0)),
                   pl.BlockSpec((1, 2 * n_chunks, SUBLANES, MXU_N), lambda b, i: (b, 0, 0, 0))],
        out_shape=[jax.ShapeDtypeStruct((nb, t, d), F32),
                   jax.ShapeDtypeStruct((nb, 2 * n_chunks, SUBLANES, MXU_N), F32)],
        scratch_shapes=[pltpu.VMEM((2, 2, SUBLANES + tm, MXU_N), F32),
                        pltpu.VMEM((2 * n_chunks, SUBLANES, MXU_N), F32),
                        pltpu.VMEM((tm, n_chunks * MXU_N), BF16)],
        compiler_params=pltpu.CompilerParams(dimension_semantics=("arbitrary", "arbitrary"),
                                             vmem_limit_bytes=VMEM_LIMIT),
        name="prompt_out",
    )(x, yab, yc, mod, p["norm2_g"], p["mix_g"], p["w_out"], p["w_up"], p["w_down_full"],
      p["ffn_taps"], final_g)


def _sample_in_body(x_ref, mod_ref, n1g_ref, win_ref, wf_ref, bf_ref, alng_ref, alnb_ref, ws0_ref, bs0_ref,
                    cw_ref, cb_ref, clng_ref, clnb_ref, mixg_ref, st_ref,
                    q_ref, k_ref, v_ref, logf_ref, yab_ref, cv_ref, nst_ref, *, a_w, b_w, c_w, conv_w):
    d = x_ref.shape[1]
    n_heads = c_w // HEAD_DIM
    h = _modulated_norm(x_ref[...], n1g_ref[...], mod_ref[:, d:2 * d], mod_ref[:, 0:d])
    hb = h.astype(BF16)
    z = _dot(hb, win_ref[...])
    u = _gelu(z[:, 0:a_w])
    vln = _layernorm(_gelu(z[:, a_w:2 * a_w]), alng_ref[...], alnb_ref[...])
    cv_ref[...] = vln
    y_a = _rms(u * (ws0_ref[...] * vln + bs0_ref[...])) * mixg_ref[:, 0:a_w]
    yab_ref[:, 0:a_w] = y_a.astype(BF16)
    o = 2 * a_w
    glu = z[:, o:o + b_w] * _sigmoid(z[:, o + b_w:o + 2 * b_w])
    hist = (conv_w - 1) * b_w
    acc = cb_ref[...] + cw_ref[conv_w - 1:conv_w, :] * glu
    for j in range(conv_w - 1):
        acc = acc + cw_ref[j:j + 1, :] * st_ref[:, j * b_w:(j + 1) * b_w]
    y_b = _rms(_silu(_layernorm(acc, clng_ref[...], clnb_ref[...]))) * mixg_ref[:, a_w:a_w + b_w]
    yab_ref[:, a_w:a_w + b_w] = y_b.astype(BF16)
    nst_ref[:, 0:hist - b_w] = st_ref[:, b_w:hist]
    nst_ref[:, hist - b_w:hist] = glu
    o += 2 * b_w
    q_ref[...] = z[:, o:o + c_w]
    k_ref[...] = z[:, o + c_w:o + 2 * c_w]
    v_ref[...] = z[:, o + 2 * c_w:o + 3 * c_w]
    logf = _log_sigmoid(_dot(hb, wf_ref[...]) + bf_ref[...])
    logf_ref[...] = logf[:, 0:n_heads]


def _sample_in(x, mod, state, p):
    n, d = x.shape
    a_w, b_w, c_w = p["a_w"], p["b_w"], p["c_w"]
    n_heads = c_w // HEAD_DIM
    conv_w = p["conv_w"].shape[0]
    body = functools.partial(_sample_in_body, a_w=a_w, b_w=b_w, c_w=c_w, conv_w=conv_w)
    outs = [((n, c_w), F32), ((n, c_w), F32), ((n, c_w), F32), ((n, n_heads), F32),
            ((n, a_w + b_w), BF16), ((n, a_w), F32), (state.shape, F32)]
    whole = lambda shape: pl.BlockSpec(shape, lambda i: (0,) * len(shape))
    args = (x, mod, p["norm1_g"], p["w_in"], p["w_f"], p["b_f"], p["a_ln_g"], p["a_ln_b"], p["w_s0"], p["b_s0"],
            p["conv_w"], p["conv_b"], p["conv_ln_g"], p["conv_ln_b"], p["mix_g"], state)
    return pl.pallas_call(
        body,
        grid=(1,),
        in_specs=[_layer_spec(a, p["layer"]) if a is p["w_in"] else whole(a.shape) for a in args],
        out_specs=[whole(s) for s, _ in outs],
        out_shape=[jax.ShapeDtypeStruct(s, dt) for s, dt in outs],
        compiler_params=pltpu.CompilerParams(dimension_semantics=("arbitrary",), vmem_limit_bytes=VMEM_LIMIT),
        name="sample_in",
    )(*args)


def _sample_attn_body(pt_ref, q_ref, kn_ref, vn_ref, fn_ref, *refs, n_pages):
    k_refs, v_refs, f_ref = refs[:n_pages], refs[n_pages:2 * n_pages], refs[2 * n_pages]
    o_ref, qb_scr, m_scr, l_scr, c_scr, s_scr, p_scr, acc_scr = refs[2 * n_pages + 1:]
    gi = pl.program_id(1)
    n_heads = qb_scr.shape[0]
    first_page = (pl.num_programs(1) - 1 - gi) * n_pages

    def pair_column(row_ref, pair):
        return _row_to_col(row_ref[0][:, pair * LANES:(pair + 1) * LANES])

    @pl.when(gi == 0)
    def _():
        for pair in range(n_heads // 2):
            q_col = pair_column(q_ref, pair) * SCALE
            for e in range(2):
                qb_scr[2 * pair + e] = jnp.broadcast_to(q_col[e * HEAD_DIM:(e + 1) * HEAD_DIM], (HEAD_DIM, PAGE))
        m_scr[...] = jnp.full_like(m_scr, NEG)
        l_scr[...] = jnp.zeros_like(l_scr)
        acc_scr[...] = jnp.zeros_like(acc_scr)
        c_scr[...] = _row_to_col(fn_ref[0])

    r = lax.broadcasted_iota(jnp.int32, (PAGE, PAGE), 0)
    c = lax.broadcasted_iota(jnp.int32, (PAGE, PAGE), 1)
    later = jnp.where(r > c, 1.0, 0.0).astype(BF16)
    logf = jnp.concatenate([f_ref[pt_ref[pl.program_id(0), first_page + i]] for i in range(n_pages)], axis=0)
    within = _dot3_r(logf, later)
    page_total = within[:, 0:1] + logf[:, 0:1]
    carry = c_scr[...]
    after = [None] * n_pages
    for i in reversed(range(n_pages)):
        after[i] = carry
        carry = carry + page_total[i * n_heads:(i + 1) * n_heads]
        for h in range(n_heads):
            s_scr[pl.ds(i * n_heads + h, 1), :] = jnp.sum(k_refs[i][0, 0, h] * qb_scr[h], axis=0, keepdims=True)
    c_scr[...] = carry
    scores = s_scr[...] + within + jnp.concatenate(after, axis=0)

    m_old = m_scr[...]
    row_max = jnp.max(scores, axis=1, keepdims=True)
    m_new = m_old
    for i in range(n_pages):
        m_new = jnp.maximum(m_new, row_max[i * n_heads:(i + 1) * n_heads])
    alpha = jnp.exp(m_old - m_new)
    pr = jnp.exp(scores - jnp.concatenate([m_new] * n_pages, axis=0))
    p_scr[...] = pr
    row_sum = jnp.sum(pr, axis=1, keepdims=True)
    l = alpha * l_scr[...]
    for i in range(n_pages):
        l = l + row_sum[i * n_heads:(i + 1) * n_heads]
    m_scr[...] = m_new
    l_scr[...] = l
    for h in range(n_heads):
        acc = alpha[h:h + 1, :] * acc_scr[h]
        for i in range(n_pages):
            acc = acc + p_scr[pl.ds(i * n_heads + h, 1), :] * v_refs[i][0, 0, h]
        acc_scr[h] = acc

    @pl.when(gi == pl.num_programs(1) - 1)
    def _():
        for pair in range(n_heads // 2):
            q_col, k_col, v_col = (pair_column(r, pair) for r in (q_ref, kn_ref, vn_ref))
            outs = []
            for e in range(2):
                h = 2 * pair + e
                own = slice(e * HEAD_DIM, (e + 1) * HEAD_DIM)
                s_new = jnp.sum(q_col[own] * SCALE * k_col[own], axis=0, keepdims=True)
                m_h = m_new[h:h + 1, :]
                m_f = jnp.maximum(m_h, s_new)
                a_f = jnp.exp(m_h - m_f)
                p_new = jnp.exp(s_new - m_f)
                l_f = a_f * l[h:h + 1, :] + p_new
                past = jnp.sum(acc_scr[h], axis=1, keepdims=True)
                outs.append((a_f * past + p_new * v_col[own]) / l_f)
            o_ref[0, :, pair * LANES:(pair + 1) * LANES] = _col_to_row(jnp.concatenate(outs, axis=0))


def _sample_attn(q, k_new, v_new, logf_new, cache_k, cache_v, cache_logf, page_table, layer):
    n, width = q.shape
    n_heads = width // HEAD_DIM
    n_seq_pages = page_table.shape[1]
    pps = min(PAGES_PER_STEP, n_seq_pages)
    n_steps = n_seq_pages // pps
    kt = cache_k.transpose(0, 1, 3, 4, 2)
    vt = cache_v.transpose(0, 1, 3, 4, 2)
    ft = cache_logf.transpose(0, 1, 3, 2)

    def page(i, *tail):
        return pl.BlockSpec((1, 1) + tail,
                            lambda b, g, pt: (layer, pt[b, (n_steps - 1 - g) * pps + i]) + (0,) * len(tail))

    row = lambda last: pl.BlockSpec((1, 1, last), lambda b, g, pt: (b, 0, 0))
    grid_spec = pltpu.PrefetchScalarGridSpec(
        num_scalar_prefetch=1,
        grid=(n, n_steps),
        in_specs=[row(width), row(width), row(width), row(n_heads)]
                 + [page(i, n_heads, HEAD_DIM, PAGE) for i in range(pps)] * 2
                 + [pl.BlockSpec((None,) + ft.shape[1:], lambda b, g, pt: (layer, 0, 0, 0),
                                 pipeline_mode=pl.Buffered(1))],
        out_specs=row(width),
        scratch_shapes=[pltpu.VMEM((n_heads, HEAD_DIM, PAGE), F32),
                        pltpu.VMEM((n_heads, 1), F32), pltpu.VMEM((n_heads, 1), F32), pltpu.VMEM((n_heads, 1), F32),
                        pltpu.VMEM((pps * n_heads, PAGE), F32), pltpu.VMEM((pps * n_heads, PAGE), F32),
                        pltpu.VMEM((n_heads, HEAD_DIM, PAGE), F32)],
    )
    as_rows = lambda a: a.reshape(n, 1, a.shape[1])
    out = pl.pallas_call(
        functools.partial(_sample_attn_body, n_pages=pps),
        grid_spec=grid_spec,
        out_shape=jax.ShapeDtypeStruct((n, 1, width), F32),
        compiler_params=pltpu.CompilerParams(dimension_semantics=("arbitrary", "arbitrary"),
                                             vmem_limit_bytes=VMEM_LIMIT),
        name="sample_attn",
    )(page_table, as_rows(q), as_rows(k_new), as_rows(v_new), as_rows(logf_new),
      *([kt] * pps), *([vt] * pps), ft)
    return out.reshape(n, width)


def _sample_out_body(x_ref, yab_ref, yc_ref, mod_ref, n2g_ref, mixg_ref, wout_ref, wg_ref, wu_ref, wd_ref,
                     taps_g_ref, taps_u_ref, s0g_ref, s0u_ref, s1g_ref, s1u_ref, fg_ref,
                     o_ref, ng_ref, nu_ref, pg_ref, pu_ref, x1_scr, hb_scr, acc_scr, *, final):
    c = pl.program_id(0)
    d = x_ref.shape[1]
    ab_w = yab_ref.shape[1]

    @pl.when(c == 0)
    def _():
        y_c = (_rms(yc_ref[...]) * mixg_ref[:, ab_w:]).astype(BF16)
        attn = _dot(yab_ref[...], wout_ref[0:ab_w, :]) + _dot(y_c, wout_ref[ab_w:, :])
        x1 = x_ref[...] + mod_ref[:, 2 * d:3 * d] * attn
        x1_scr[...] = x1
        hb_scr[...] = _modulated_norm(x1, n2g_ref[...], mod_ref[:, 4 * d:5 * d], mod_ref[:, 3 * d:4 * d]).astype(BF16)
        acc_scr[...] = jnp.zeros_like(acc_scr)

    tg, tu = taps_g_ref[0], taps_u_ref[0]
    g, u, down = _ffn_chunk(hb_scr[...], wg_ref[...], wu_ref[...], wd_ref[...],
                            [tg[i:i + 1] for i in range(4)], [tu[i:i + 1] for i in range(4)],
                            (s0g_ref[...], s1g_ref[...]), (s0u_ref[...], s1u_ref[...]))
    ng_ref[...] = g
    nu_ref[...] = u
    pg_ref[...] = s1g_ref[...]
    pu_ref[...] = s1u_ref[...]
    acc_scr[...] += down

    @pl.when(c == pl.num_programs(0) - 1)
    def _():
        x2 = x1_scr[...] + mod_ref[:, 5 * d:6 * d] * acc_scr[...]
        o_ref[...] = _rms(x2) * fg_ref[...] if final else x2


def _sample_out(x, yab, yc, mod, state, p, final_g, final):
    n, d = x.shape
    n_chunks = p["w_up"].shape[2] // (2 * MXU_N)
    d_ff = n_chunks * MXU_N
    const = lambda *shape: pl.BlockSpec(shape, lambda c: (0,) * len(shape))
    st = lambda off: pl.BlockSpec((n, MXU_N), lambda c: (0, off + c))
    col = pl.BlockSpec((n, MXU_N), lambda c: (0, c))
    outs = pl.pallas_call(
        functools.partial(_sample_out_body, final=final),
        grid=(n_chunks,),
        in_specs=[const(n, d), const(*yab.shape), const(*yc.shape), const(*mod.shape), const(1, d), const(1, d),
                  _layer_spec(p["w_out"], p["layer"]),
                  pl.BlockSpec((None, d, MXU_N), lambda c: (p["layer"], 0, c)),
                  pl.BlockSpec((None, d, MXU_N), lambda c: (p["layer"], 0, n_chunks + c)),
                  pl.BlockSpec((None, MXU_N, d), lambda c: (p["layer"], c, 0)),
                  pl.BlockSpec((1, SUBLANES, MXU_N), lambda c: (c, 0, 0)),
                  pl.BlockSpec((1, SUBLANES, MXU_N), lambda c: (n_chunks + c, 0, 0)),
                  st(0), st(n_chunks), st(2 * n_chunks), st(3 * n_chunks), const(1, d)],
        out_specs=[const(n, d), col, col, col, col],
        out_shape=[jax.ShapeDtypeStruct((n, d), F32)] + [jax.ShapeDtypeStruct((n, d_ff), F32)] * 4,
        scratch_shapes=[pltpu.VMEM((n, d), F32), pltpu.VMEM((n, d), BF16), pltpu.VMEM((n, d), F32)],
        compiler_params=pltpu.CompilerParams(dimension_semantics=("arbitrary",), vmem_limit_bytes=VMEM_LIMIT),
        name="sample_out",
    )(x, yab, yc, mod, p["norm2_g"], p["mix_g"], p["w_out"], p["w_up"], p["w_up"], p["w_down_full"],
      p["ffn_taps"], p["ffn_taps"], state, state, state, state, final_g)
    x2, new_g, new_u, prev_g, prev_u = outs
    new_state = jnp.stack([jnp.concatenate([prev_g, prev_u], axis=1), jnp.concatenate([new_g, new_u], axis=1)], axis=1)
    return x2, new_state


def _mxu_weights(w_in, w_out, w_up, w_down, main, c_w):
    return dict(w_in=w_in[:, :, :main].astype(BF16),
                w_vt=w_in[:, :, main - c_w:main].transpose(0, 2, 1).astype(BF16),
                w_out=w_out.astype(BF16), w_up=w_up.astype(BF16), w_down_full=w_down.astype(BF16))


def _layer_params(l, d, mxu, norm1_g, norm2_g, w_in, b_forget, a_ln_g, a_ln_b, w_s, b_s, conv_w, conv_b,
                  conv_ln_g, conv_ln_b, mix_g, ffn_conv_w, ffn_conv_b):
    a_w, b_w, n_heads = a_ln_g.shape[1], conv_b.shape[1], b_forget.shape[1]
    c_w = n_heads * HEAD_DIM
    main = 2 * a_w + 2 * b_w + 3 * c_w
    d_ff = mxu["w_down_full"].shape[1]
    n_chunks = d_ff // MXU_N
    row = lambda v: v.reshape(1, -1)
    pad_lanes = lambda a: jnp.pad(a, ((0, 0), (0, LANES - a.shape[1])))
    decay_q, decay_k = _decay_placement(n_heads)
    chunked = lambda v: v.reshape(2 * n_chunks, MXU_N)
    taps = jnp.stack([chunked(ffn_conv_w[l, 0]), chunked(ffn_conv_w[l, 1]), chunked(ffn_conv_w[l, 2]),
                      chunked(ffn_conv_b[l])], axis=1)
    taps = jnp.pad(taps, ((0, 0), (0, SUBLANES - 4), (0, 0)))
    return dict(
        mxu, layer=l,
        a_w=a_w, b_w=b_w, c_w=c_w,
        norm1_g=row(norm1_g[l]), norm2_g=row(norm2_g[l]),
        w_f=pad_lanes(jnp.tile(w_in[l, :, main:], (1, 3))).astype(BF16),
        b_f=pad_lanes(jnp.tile(row(b_forget[l]), (1, 3))),
        decay_q=decay_q, decay_k=decay_k,
        a_ln_g=row(a_ln_g[l]), a_ln_b=row(a_ln_b[l]),
        w_s=w_s[l], b_s_full=jnp.repeat(b_s[l].T, HEAD_DIM, axis=1),
        w_s0=row(jnp.repeat(w_s[l, :, 0, 0], HEAD_DIM)), b_s0=row(jnp.repeat(b_s[l, :, 0], HEAD_DIM)),
        conv_w=conv_w[l], conv_b=row(conv_b[l]), conv_ln_g=row(conv_ln_g[l]), conv_ln_b=row(conv_ln_b[l]),
        mix_g=row(mix_g[l]),
        ffn_taps=taps,
    )


def kernel(x_prompt, x_sample, cache_k, cache_v, cache_logf, state_conv, state_ffn_conv, page_table, c_prompt, c_sample, norm1_g, norm2_g, w_ada, b_ada, w_in, b_forget, a_ln_g, a_ln_b, w_s, b_s, conv_w, conv_b, conv_ln_g, conv_ln_b, mix_g, w_out, w_up, ffn_conv_w, ffn_conv_b, w_down, final_g):
    nb, t, d = x_prompt.shape
    ns = x_sample.shape[0]
    depth = w_in.shape[0]
    n_heads = b_forget.shape[1]
    tm = min(ROW_TILE, t)

    c_all = jnp.concatenate([c_prompt, c_sample], axis=0)
    pad_rows = (-c_all.shape[0]) % 16
    mod = _ada(jnp.pad(c_all, ((0, pad_rows), (0, 0))), w_ada, b_ada)
    fg = final_g.reshape(1, d)

    c_w = n_heads * HEAD_DIM
    mxu = _mxu_weights(w_in, w_out, w_up, w_down, w_in.shape[2] - n_heads, c_w)
    xp, xs = x_prompt, x_sample.reshape(ns, d)
    prompt_out, sample_out = [], []
    for l in range(depth):
        p = _layer_params(l, d, mxu, norm1_g, norm2_g, w_in, b_forget, a_ln_g, a_ln_b, w_s, b_s, conv_w, conv_b,
                          conv_ln_g, conv_ln_b, mix_g, ffn_conv_w, ffn_conv_b)
        n_chunks = p["w_up"].shape[2] // (2 * MXU_N)
        final = l == depth - 1
        mod_p = mod[l, :nb].reshape(nb, 6, d)
        mod_s = mod[l, nb:nb + ns]

        k, vt, logf, qh, kh, vh, yab, conv_st = _prompt_in(xp, mod_p, p, tm)
        yc = _prompt_attn(qh, kh, vh, tm)
        xp, ffn_tail = _prompt_out(xp, yab, yc, mod_p, p, fg, tm, final)
        ffn_st = ffn_tail[:, :, SUBLANES - 2:, :].transpose(0, 2, 1, 3).reshape(nb, 2, 2 * n_chunks * MXU_N)
        v_rows = vt.reshape(nb, n_heads, HEAD_DIM, t).transpose(0, 3, 1, 2)
        prompt_out.append((k.reshape(nb, t, n_heads, HEAD_DIM), v_rows, logf, conv_st, ffn_st))

        st_conv = state_conv[l].reshape(ns, -1)
        q_s, k_s, v_s, logf_s, yab_s, chunk_v, new_conv = _sample_in(xs, mod_s, st_conv, p)
        yc_s = _sample_attn(q_s, k_s, v_s, logf_s, cache_k, cache_v, cache_logf, page_table, l)
        xs, new_ffn = _sample_out(xs, yab_s, yc_s, mod_s, state_ffn_conv[l].reshape(ns, -1), p, fg, final)
        sample_out.append((k_s.reshape(ns, 1, n_heads, HEAD_DIM), v_s.reshape(ns, 1, n_heads, HEAD_DIM),
                           logf_s.reshape(ns, 1, n_heads), new_conv.reshape(state_conv.shape[1:]), new_ffn,
                           chunk_v.reshape(ns, 1, -1)))

    stack = lambda outs, i: jnp.stack([o[i] for o in outs])
    return (xp, xs.reshape(ns, 1, d),
            stack(prompt_out, 0), stack(prompt_out, 1), stack(prompt_out, 2), stack(prompt_out, 3), stack(prompt_out, 4),
            stack(sample_out, 0), stack(sample_out, 1), stack(sample_out, 2), stack(sample_out, 3), stack(sample_out, 4),
            stack(sample_out, 5))
```

```python
import functools

import numpy as np

import jax
import jax.numpy as jnp
from jax import lax
from jax.experimental import pallas as pl
from jax.experimental.pallas import tpu as pltpu

F32 = jnp.float32
BF16 = jnp.bfloat16

HEAD_DIM = 64
CHUNK = 128
PAGE = 128
EPS = 1e-6
SCALE = HEAD_DIM ** -0.5
NEG = -1e30
LOG2E = 1.4426950408889634

LANES = 128
SUBLANES = 8
MXU_N = 256
VMEM_LIMIT = 56 * 1024 * 1024

ROW_TILE = 512
CONV_ROWS = 64
ATTN_ROWS = 32
HALO = 32
PAGES_PER_STEP = 32


def _dot(a, b):
    return jnp.dot(a, b, preferred_element_type=F32)


def _dot_nt(a, b):
    return lax.dot_general(a, b, (((1,), (1,)), ((), ())), preferred_element_type=F32)


def _rms(x):
    return x * lax.rsqrt(jnp.mean(x * x, axis=-1, keepdims=True) + EPS)


def _layernorm(x, g, b):
    xc = x - jnp.mean(x, axis=-1, keepdims=True)
    return xc * lax.rsqrt(jnp.mean(xc * xc, axis=-1, keepdims=True) + EPS) * g + b


def _gelu(x):
    return 0.5 * x * (1.0 + lax.erf(x * (2.0 ** -0.5)))


def _sigmoid(x):
    return 1.0 / (1.0 + jnp.exp(-x))


def _silu(x):
    return x * _sigmoid(x)


def _log_sigmoid(x):
    return jnp.minimum(x, 0.0) - jnp.log(1.0 + jnp.exp(-jnp.abs(x)))


def _split3(x):
    hi = x.astype(BF16)
    r = x - hi.astype(F32)
    mid = r.astype(BF16)
    lo = (r - mid.astype(F32)).astype(BF16)
    return hi, mid, lo


def _dot3(w01, x):
    hi, mid, lo = _split3(x)
    return _dot(w01, hi) + _dot(w01, mid) + _dot(w01, lo)


def _dot3_r(x, w01):
    hi, mid, lo = _split3(x)
    return _dot(hi, w01) + _dot(mid, w01) + _dot(lo, w01)


def _diagonal(vec, n):
    r = lax.broadcasted_iota(jnp.int32, (n, n), 0)
    c = lax.broadcasted_iota(jnp.int32, (n, n), 1)
    return jnp.where(r == c, jnp.broadcast_to(vec, (n, n)), 0.0)


def _row_to_col(row):
    return jnp.sum(_diagonal(row, row.shape[1]), axis=1, keepdims=True)


def _col_to_row(col):
    return jnp.sum(_diagonal(col, col.shape[0]), axis=0, keepdims=True)


def _modulated_norm(x, g, scale, shift):
    return _rms(x) * g * (1.0 + scale) + shift


def _ada_body(c_ref, w_ref, b_ref, o_ref):
    c = c_ref[...]
    o_ref[0] = _dot(_silu(c).astype(BF16), w_ref[0].astype(BF16)) + b_ref[0]


def _ada(c_all, w_ada, b_ada):
    depth, d, n = w_ada.shape
    rows = c_all.shape[0]
    tn = n // 4
    return pl.pallas_call(
        _ada_body,
        grid=(depth, n // tn),
        in_specs=[pl.BlockSpec((rows, d), lambda l, j: (0, 0)),
                  pl.BlockSpec((1, d, tn), lambda l, j: (l, 0, j)),
                  pl.BlockSpec((1, 1, tn), lambda l, j: (l, 0, j))],
        out_specs=pl.BlockSpec((1, rows, tn), lambda l, j: (l, 0, j)),
        out_shape=jax.ShapeDtypeStruct((depth, rows, n), F32),
        compiler_params=pltpu.CompilerParams(dimension_semantics=("arbitrary", "arbitrary"),
                                             vmem_limit_bytes=VMEM_LIMIT),
        name="ada_mod",
    )(c_all, w_ada, b_ada.reshape(depth, 1, n))


def _own_half(rows, h):
    lane = lax.broadcasted_iota(jnp.int32, (rows, LANES), 1)
    parity = h % 2
    return lane, (lane >= HEAD_DIM * parity) & (lane < HEAD_DIM * (parity + 1))


def _decay_placement(n_heads):
    pq = np.zeros((LANES, n_heads * LANES), np.float32)
    pk = np.zeros((LANES, n_heads * LANES), np.float32)
    one_lane = 3 * n_heads
    for h in range(n_heads):
        base = h * LANES + HEAD_DIM * (1 - h % 2)
        for i in range(3):
            pq[i * n_heads + h, base + i] = 1.0
            pq[one_lane, base + 3 + i] = 1.0
            pk[one_lane, base + i] = 1.0
            pk[i * n_heads + h, base + 3 + i] = -1.0
    return jnp.asarray(pq, BF16), jnp.asarray(pk, BF16)


def _prompt_in_body(x_ref, mod_ref, n1g_ref, win_ref, wvt_ref, wf_ref, bf_ref, alng_ref, alnb_ref, ws_ref, bsf_ref,
                    cw_ref, cb_ref, clng_ref, clnb_ref, mixg_ref, tri_ref, pq_ref, pk_ref,
                    k_ref, vt_ref, logf_ref, qh_ref, kh_ref, vh_ref, yab_ref, cst_ref,
                    xp_scr, xs_scr, carry_scr, *, a_w, b_w, c_w, conv_w):
    t = pl.program_id(1)
    tm = x_ref.shape[1]
    n_heads = c_w // HEAD_DIM

    @pl.when(t == 0)
    def _():
        xp_scr[0:HALO, :] = jnp.zeros((HALO, b_w), F32)
        carry_scr[...] = jnp.zeros_like(carry_scr)

    x = x_ref[0]
    h = _modulated_norm(x, n1g_ref[...], mod_ref[0, 1:2, :], mod_ref[0, 0:1, :])
    hb = h.astype(BF16)

    z_uv = _dot(hb, win_ref[:, 0:2 * a_w])
    u = _gelu(z_uv[:, :a_w])
    vln = _layernorm(_gelu(z_uv[:, a_w:]), alng_ref[...], alnb_ref[...]).astype(BF16)
    row = lax.broadcasted_iota(jnp.int32, (CHUNK, CHUNK), 0)
    col = lax.broadcasted_iota(jnp.int32, (CHUNK, CHUNK), 1)
    lane_head = lax.broadcasted_iota(jnp.int32, (CHUNK, a_w), 1) // HEAD_DIM
    w_tril = [jnp.where(col <= row, ws_ref[hh], 0.0).astype(BF16) for hh in range(a_w // HEAD_DIM)]
    sv_chunks = []
    for c in range(tm // CHUNK):
        vc = vln[c * CHUNK:(c + 1) * CHUNK]
        sv = bsf_ref[...]
        for hh, w in enumerate(w_tril):
            sv = sv + jnp.where(lane_head == hh, _dot(w, vc), 0.0)
        sv_chunks.append(sv)
    y_a = _rms(u * jnp.concatenate(sv_chunks, axis=0)) * mixg_ref[:, 0:a_w]
    yab_ref[0, :, 0:a_w] = y_a.astype(BF16)

    z_ag = _dot(hb, win_ref[:, 2 * a_w:2 * a_w + 2 * b_w])
    xp_scr[HALO:HALO + tm, :] = z_ag[:, :b_w] * _sigmoid(z_ag[:, b_w:])
    shifted_rows = xs_scr.shape[1]
    for r in range(1, SUBLANES):
        xs_scr[r - 1] = xp_scr[pl.ds(r, shifted_rows), :]
    first_tap = HALO - (conv_w - 1)
    conv_blocks = []
    for rb in range(tm // CONV_ROWS):
        acc = jnp.broadcast_to(cb_ref[...], (CONV_ROWS, b_w))
        for j in range(conv_w):
            whole, r = divmod(first_tap + j, SUBLANES)
            src = xp_scr if r == 0 else xs_scr.at[r - 1]
            acc = acc + cw_ref[j:j + 1, :] * src[pl.ds(rb * CONV_ROWS + whole * SUBLANES, CONV_ROWS), :]
        conv_blocks.append(acc)
    conv = jnp.concatenate(conv_blocks, axis=0)
    y_b = _rms(_silu(_layernorm(conv, clng_ref[...], clnb_ref[...]))) * mixg_ref[:, a_w:a_w + b_w]
    yab_ref[0, :, a_w:a_w + b_w] = y_b.astype(BF16)
    cst_ref[0] = xp_scr[pl.ds(HALO + tm - (conv_w - 1), conv_w - 1), :]
    xp_scr[0:HALO, :] = xp_scr[tm:tm + HALO, :]

    o = 2 * a_w + 2 * b_w
    z_q = _dot(hb, win_ref[:, o:o + c_w]) * (SCALE * LOG2E)
    z_k = _dot(hb, win_ref[:, o + c_w:o + 2 * c_w])
    z_vt = _dot_nt(wvt_ref[...], hb)
    k_ref[0] = z_k
    vt_ref[0] = z_vt
    logf = _log_sigmoid(_dot(hb, wf_ref[...]) + bf_ref[...])
    logf_ref[0] = logf[:, 0:n_heads]
    cum = _dot3(tri_ref[...], logf) + carry_scr[...]
    carry_scr[...] = cum[tm - 1:tm, :]
    hi, mid, lo = _split3(cum * LOG2E)
    lane = lax.broadcasted_iota(jnp.int32, (tm, LANES), 1)
    one = jnp.where(lane == 3 * n_heads, 1.0, 0.0).astype(BF16)
    packed = jnp.where(lane < n_heads, hi, jnp.where(lane < 2 * n_heads, mid, jnp.where(lane < 3 * n_heads, lo, one)))
    decay_q = _dot(packed, pq_ref[...])
    decay_k = _dot(packed, pk_ref[...])
    for hh in range(n_heads):
        slab = slice((hh // 2) * LANES, (hh // 2 + 1) * LANES)
        mine = slice(hh * LANES, (hh + 1) * LANES)
        own = _own_half(tm, hh)[1]
        qh_ref[0, hh] = jnp.where(own, z_q[:, slab], decay_q[:, mine]).astype(BF16)
        kh_ref[0, hh] = jnp.where(own, z_k[:, slab], decay_k[:, mine]).astype(BF16)
        feat = lax.broadcasted_iota(jnp.int32, (LANES, tm), 0) // HEAD_DIM
        vh_ref[0, hh, 0] = jnp.where(feat == hh % 2, z_vt[slab, :], 1.0).astype(BF16)


def _prompt_in(x, mod, p, tm):
    nb, t, d = x.shape
    a_w, b_w, c_w = p["a_w"], p["b_w"], p["c_w"]
    n_heads = c_w // HEAD_DIM
    conv_w = p["conv_w"].shape[0]
    const = lambda *shape: pl.BlockSpec(shape, lambda b, i: (0,) * len(shape))
    body = functools.partial(_prompt_in_body, a_w=a_w, b_w=b_w, c_w=c_w, conv_w=conv_w)
    return pl.pallas_call(
        body,
        grid=(nb, t // tm),
        in_specs=[pl.BlockSpec((1, tm, d), lambda b, i: (b, i, 0)),
                  pl.BlockSpec((1, 6, d), lambda b, i: (b, 0, 0)),
                  const(1, d), _layer_spec(p["w_in"], p["layer"]), _layer_spec(p["w_vt"], p["layer"]),
                  const(d, LANES), const(1, LANES),
                  const(1, a_w), const(1, a_w), const(*p["w_s"].shape), const(CHUNK, a_w),
                  const(*p["conv_w"].shape), const(1, b_w), const(1, b_w), const(1, b_w), const(1, d),
                  const(tm, tm), const(*p["decay_q"].shape), const(*p["decay_k"].shape)],
        out_specs=[pl.BlockSpec((1, tm, c_w), lambda b, i: (b, i, 0)),
                   pl.BlockSpec((1, c_w, tm), lambda b, i: (b, 0, i)),
                   pl.BlockSpec((1, tm, n_heads), lambda b, i: (b, i, 0)),
                   pl.BlockSpec((1, n_heads, tm, LANES), lambda b, i: (b, 0, i, 0)),
                   pl.BlockSpec((1, n_heads, tm, LANES), lambda b, i: (b, 0, i, 0)),
                   pl.BlockSpec((1, n_heads, 1, LANES, tm), lambda b, i: (b, 0, i, 0, 0)),
                   pl.BlockSpec((1, tm, a_w + b_w), lambda b, i: (b, i, 0)),
                   pl.BlockSpec((1, conv_w - 1, b_w), lambda b, i: (b, 0, 0))],
        out_shape=[jax.ShapeDtypeStruct((nb, t, c_w), F32),
                   jax.ShapeDtypeStruct((nb, c_w, t), F32),
                   jax.ShapeDtypeStruct((nb, t, n_heads), F32),
                   jax.ShapeDtypeStruct((nb, n_heads, t, LANES), BF16),
                   jax.ShapeDtypeStruct((nb, n_heads, t, LANES), BF16),
                   jax.ShapeDtypeStruct((nb, n_heads, t // tm, LANES, tm), BF16),
                   jax.ShapeDtypeStruct((nb, t, a_w + b_w), BF16),
                   jax.ShapeDtypeStruct((nb, conv_w - 1, b_w), F32)],
        scratch_shapes=[pltpu.VMEM((HALO + tm, b_w), F32),
                        pltpu.VMEM((SUBLANES - 1, HALO + tm - SUBLANES, b_w), F32),
                        pltpu.VMEM((1, LANES), F32)],
        compiler_params=pltpu.CompilerParams(dimension_semantics=("arbitrary", "arbitrary"),
                                             vmem_limit_bytes=VMEM_LIMIT),
        name="prompt_in",
    )(x, mod, p["norm1_g"], p["w_in"], p["w_vt"], p["w_f"], p["b_f"], p["a_ln_g"], p["a_ln_b"], p["w_s"],
      p["b_s_full"], p["conv_w"], p["conv_b"], p["conv_ln_g"], p["conv_ln_b"], p["mix_g"],
      jnp.tril(jnp.ones((tm, tm), BF16)), p["decay_q"], p["decay_k"])


def _softmax_keys_on_rows(s_ref, p_ref, m_ref, a_ref, first_key):
    tk, tq = s_ref.shape

    def visible(r0, c0):
        if first_key is None or first_key + r0 + ATTN_ROWS - 1 <= c0:
            return "all"
        return "none" if first_key + r0 > c0 + LANES - 1 else "some"

    def chunk(r0, c0):
        sc = s_ref[r0:r0 + ATTN_ROWS, c0:c0 + LANES]
        if visible(r0, c0) == "some":
            key = first_key + r0 + lax.broadcasted_iota(jnp.int32, (ATTN_ROWS, LANES), 0)
            qry = c0 + lax.broadcasted_iota(jnp.int32, (ATTN_ROWS, LANES), 1)
            sc = jnp.where(key <= qry, sc, NEG)
        return sc

    for c0 in range(0, tq, LANES):
        cols = slice(c0, c0 + LANES)
        seen = [r0 for r0 in range(0, tk, ATTN_ROWS) if visible(r0, c0) != "none"]
        if not seen:
            a_ref[:, cols] = jnp.ones((1, LANES), F32)
            p_ref[:, cols] = jnp.zeros((tk, LANES), BF16)
            continue
        top = chunk(seen[0], c0)
        for r0 in seen[1:]:
            top = jnp.maximum(top, chunk(r0, c0))
        m_old = m_ref[:, cols]
        m_new = jnp.maximum(m_old, jnp.max(top, axis=0, keepdims=True))
        m_ref[:, cols] = m_new
        a_ref[:, cols] = jnp.exp2(m_old - m_new)
        for r0 in range(0, tk, ATTN_ROWS):
            if r0 in seen:
                p_ref[r0:r0 + ATTN_ROWS, cols] = jnp.exp2((chunk(r0, c0) - m_new).astype(BF16))
            else:
                p_ref[r0:r0 + ATTN_ROWS, cols] = jnp.zeros((ATTN_ROWS, LANES), BF16)


def _prompt_attn_body(q_ref, k_ref, v_ref, o_ref, s_scr, p_scr, m_scr, a_scr, acc_scr):
    i = pl.program_id(2)
    tq = q_ref.shape[2]
    m_scr[...] = jnp.full_like(m_scr, NEG)
    acc_scr[...] = jnp.zeros_like(acc_scr)

    tk = tq // 2

    def scores(j, half):
        start = pl.multiple_of(j * tq + half * tk, tk)
        for e in range(2):
            s_scr[half, e] = _dot_nt(k_ref[0, e, pl.ds(start, tk), :], q_ref[0, e])

    def consume(j, half, masked):
        for e in range(2):
            _softmax_keys_on_rows(s_scr.at[half, e], p_scr.at[half, e], m_scr.at[e], a_scr.at[e],
                                  half * tk if masked else None)
            acc_scr[e] = a_scr[e] * acc_scr[e] + _dot(v_ref[0, e, j, :, half * tk:(half + 1) * tk], p_scr[half, e])

    def past_block(j):
        scores(j, 1)
        consume(j, 0, False)
        scores(j + 1, 0)
        consume(j, 1, False)

    def past_pair(jj, _):
        past_block(2 * jj)
        past_block(2 * jj + 1)
        return 0

    scores(0, 0)
    lax.fori_loop(0, i // 2, past_pair, 0)

    @pl.when(i % 2 == 1)
    def _():
        past_block(i - 1)

    scores(i, 1)
    consume(i, 0, True)
    consume(i, 1, True)
    halves = []
    for e in range(2):
        own, other = slice(e * HEAD_DIM, (e + 1) * HEAD_DIM), slice((1 - e) * HEAD_DIM, (2 - e) * HEAD_DIM)
        halves.append(acc_scr[e, own, :] / acc_scr[e, other, :])
    o_ref[0] = jnp.concatenate(halves, axis=0).T


def _prompt_attn(qh, kh, vh, tq):
    nb, n_heads, t, _ = qh.shape
    pair = lambda rows, imap: pl.BlockSpec((1, 2, rows, LANES), imap)
    return pl.pallas_call(
        _prompt_attn_body,
        grid=(nb, n_heads // 2, t // tq),
        in_specs=[pair(tq, lambda b, j, i: (b, j, i, 0)),
                  pair(t, lambda b, j, i: (b, j, 0, 0)),
                  pl.BlockSpec((1, 2, t // tq, LANES, tq), lambda b, j, i: (b, j, 0, 0, 0))],
        out_specs=pl.BlockSpec((1, tq, LANES), lambda b, j, i: (b, i, j)),
        out_shape=jax.ShapeDtypeStruct((nb, t, n_heads * HEAD_DIM), F32),
        scratch_shapes=[pltpu.VMEM((2, 2, tq // 2, tq), F32), pltpu.VMEM((2, 2, tq // 2, tq), BF16),
                        pltpu.VMEM((2, 1, tq), F32), pltpu.VMEM((2, 1, tq), F32), pltpu.VMEM((2, LANES, tq), F32)],
        compiler_params=pltpu.CompilerParams(dimension_semantics=("arbitrary", "arbitrary", "arbitrary"),
                                             vmem_limit_bytes=VMEM_LIMIT),
        name="prompt_attn",
    )(qh, kh, vh)


def _ffn_chunk(hb, wg, wu, wd, taps_g, taps_u, prev_g, prev_u):
    g = _dot(hb, wg)
    u = _dot(hb, wu)
    cg = taps_g[0] * prev_g[0] + taps_g[1] * prev_g[1] + taps_g[2] * g + taps_g[3]
    cu = taps_u[0] * prev_u[0] + taps_u[1] * prev_u[1] + taps_u[2] * u + taps_u[3]
    return g, u, _dot((_silu(cg) * cu).astype(BF16), wd)


def _prompt_out_body(x_ref, yab_ref, yc_ref, mod_ref, n2g_ref, mixg_ref, wout_ref, wup_ref, wd_ref,
                     taps_ref, fg_ref, o_ref, fst_ref, buf_scr, halo_scr, act_scr, *, final):
    t = pl.program_id(1)
    tm = x_ref.shape[1]
    ab_w = yab_ref.shape[2]
    n_chunks = wup_ref.shape[1] // (2 * MXU_N)

    @pl.when(t == 0)
    def _():
        halo_scr[...] = jnp.zeros_like(halo_scr)

    y_c = (_rms(yc_ref[0]) * mixg_ref[:, ab_w:]).astype(BF16)
    attn = _dot(yab_ref[0], wout_ref[0:ab_w, :]) + _dot(y_c, wout_ref[ab_w:, :])
    x1 = x_ref[0] + mod_ref[0, 2:3, :] * attn
    hb = _modulated_norm(x1, n2g_ref[...], mod_ref[0, 4:5, :], mod_ref[0, 3:4, :]).astype(BF16)

    for c in range(n_chunks):
        conv = []
        for gu in range(2):
            slot = gu * n_chunks + c
            buf = buf_scr.at[c % 2, gu]
            cur = _dot(hb, wup_ref[:, slot * MXU_N:(slot + 1) * MXU_N])
            buf[0:SUBLANES, :] = halo_scr[slot]
            buf[SUBLANES:SUBLANES + tm, :] = cur
            tp = taps_ref[slot]
            conv.append(tp[0:1] * buf[pl.ds(SUBLANES - 2, tm), :] + tp[1:2] * buf[pl.ds(SUBLANES - 1, tm), :]
                        + tp[2:3] * cur + tp[3:4])
            tail = buf[tm:tm + SUBLANES, :]
            halo_scr[slot] = tail
            fst_ref[0, slot] = tail
        act_scr[:, c * MXU_N:(c + 1) * MXU_N] = (_silu(conv[0]) * conv[1]).astype(BF16)

    x2 = x1 + mod_ref[0, 5:6, :] * _dot(act_scr[...], wd_ref[...])
    o_ref[0] = _rms(x2) * fg_ref[...] if final else x2


def _resident(shape):
    return pl.BlockSpec(shape, lambda *_: (0,) * len(shape), pipeline_mode=pl.Buffered(1))


def _layer_spec(arr, layer, resident=False):
    mode = dict(pipeline_mode=pl.Buffered(1)) if resident else {}
    return pl.BlockSpec((None,) + arr.shape[1:], lambda *_: (layer,) + (0,) * (arr.ndim - 1), **mode)


def _prompt_out(x, yab, yc, mod, p, final_g, tm, final):
    nb, t, d = x.shape
    n_chunks = p["w_up"].shape[2] // (2 * MXU_N)
    body = functools.partial(_prompt_out_body, final=final)
    const = lambda *shape: pl.BlockSpec(shape, lambda b, i: (0,) * len(shape))
    return pl.pallas_call(
        body,
        grid=(nb, t // tm),
        in_specs=[pl.BlockSpec((1, tm, d), lambda b, i: (b, i, 0)),
                  pl.BlockSpec((1, tm, yab.shape[2]), lambda b, i: (b, i, 0)),
                  pl.BlockSpec((1, tm, yc.shape[2]), lambda b, i: (b, i, 0)),
                  pl.BlockSpec((1, 6, d), lambda b, i: (b, 0, 0)),
                  const(1, d), const(1, d),
                  _layer_spec(p["w_out"], p["layer"], True), _layer_spec(p["w_up"], p["layer"], True),
                  _layer_spec(p["w_down_full"], p["layer"], True), _resident(p["ffn_taps"].shape), const(1, d)],
        out_specs=[pl.BlockSpec((1, tm, d), lambda b, i: (b, i, 0)),
                   pl.BlockSpec((1, 2 * n_chunks, SUBLANES, MXU_N), lambda b, i: (b, 0, 0, 0))],
        out_shape=[jax.ShapeDtypeStruct((nb, t, d), F32),
                   jax.ShapeDtypeStruct((nb, 2 * n_chunks, SUBLANES, MXU_N), F32)],
        scratch_shapes=[pltpu.VMEM((2, 2, SUBLANES + tm, MXU_N), F32),
                        pltpu.VMEM((2 * n_chunks, SUBLANES, MXU_N), F32),
                        pltpu.VMEM((tm, n_chunks * MXU_N), BF16)],
        compiler_params=pltpu.CompilerParams(dimension_semantics=("arbitrary", "arbitrary"),
                                             vmem_limit_bytes=VMEM_LIMIT),
        name="prompt_out",
    )(x, yab, yc, mod, p["norm2_g"], p["mix_g"], p["w_out"], p["w_up"], p["w_down_full"],
      p["ffn_taps"], final_g)


def _sample_in_body(x_ref, mod_ref, n1g_ref, win_ref, wf_ref, bf_ref, alng_ref, alnb_ref, ws0_ref, bs0_ref,
                    cw_ref, cb_ref, clng_ref, clnb_ref, mixg_ref, st_ref,
                    q_ref, k_ref, v_ref, logf_ref, yab_ref, cv_ref, nst_ref, *, a_w, b_w, c_w, conv_w):
    d = x_ref.shape[1]
    n_heads = c_w // HEAD_DIM
    h = _modulated_norm(x_ref[...], n1g_ref[...], mod_ref[:, d:2 * d], mod_ref[:, 0:d])
    hb = h.astype(BF16)
    z = _dot(hb, win_ref[...])
    u = _gelu(z[:, 0:a_w])
    vln = _layernorm(_gelu(z[:, a_w:2 * a_w]), alng_ref[...], alnb_ref[...])
    cv_ref[...] = vln
    y_a = _rms(u * (ws0_ref[...] * vln + bs0_ref[...])) * mixg_ref[:, 0:a_w]
    yab_ref[:, 0:a_w] = y_a.astype(BF16)
    o = 2 * a_w
    glu = z[:, o:o + b_w] * _sigmoid(z[:, o + b_w:o + 2 * b_w])
    hist = (conv_w - 1) * b_w
    acc = cb_ref[...] + cw_ref[conv_w - 1:conv_w, :] * glu
    for j in range(conv_w - 1):
        acc = acc + cw_ref[j:j + 1, :] * st_ref[:, j * b_w:(j + 1) * b_w]
    y_b = _rms(_silu(_layernorm(acc, clng_ref[...], clnb_ref[...]))) * mixg_ref[:, a_w:a_w + b_w]
    yab_ref[:, a_w:a_w + b_w] = y_b.astype(BF16)
    nst_ref[:, 0:hist - b_w] = st_ref[:, b_w:hist]
    nst_ref[:, hist - b_w:hist] = glu
    o += 2 * b_w
    q_ref[...] = z[:, o:o + c_w]
    k_ref[...] = z[:, o + c_w:o + 2 * c_w]
    v_ref[...] = z[:, o + 2 * c_w:o + 3 * c_w]
    logf = _log_sigmoid(_dot(hb, wf_ref[...]) + bf_ref[...])
    logf_ref[...] = logf[:, 0:n_heads]


def _sample_in(x, mod, state, p):
    n, d = x.shape
    a_w, b_w, c_w = p["a_w"], p["b_w"], p["c_w"]
    n_heads = c_w // HEAD_DIM
    conv_w = p["conv_w"].shape[0]
    body = functools.partial(_sample_in_body, a_w=a_w, b_w=b_w, c_w=c_w, conv_w=conv_w)
    outs = [((n, c_w), F32), ((n, c_w), F32), ((n, c_w), F32), ((n, n_heads), F32),
            ((n, a_w + b_w), BF16), ((n, a_w), F32), (state.shape, F32)]
    whole = lambda shape: pl.BlockSpec(shape, lambda i: (0,) * len(shape))
    args = (x, mod, p["norm1_g"], p["w_in"], p["w_f"], p["b_f"], p["a_ln_g"], p["a_ln_b"], p["w_s0"], p["b_s0"],
            p["conv_w"], p["conv_b"], p["conv_ln_g"], p["conv_ln_b"], p["mix_g"], state)
    return pl.pallas_call(
        body,
        grid=(1,),
        in_specs=[_layer_spec(a, p["layer"]) if a is p["w_in"] else whole(a.shape) for a in args],
        out_specs=[whole(s) for s, _ in outs],
        out_shape=[jax.ShapeDtypeStruct(s, dt) for s, dt in outs],
        compiler_params=pltpu.CompilerParams(dimension_semantics=("arbitrary",), vmem_limit_bytes=VMEM_LIMIT),
        name="sample_in",
    )(*args)


def _sample_attn_body(pt_ref, q_ref, kn_ref, vn_ref, fn_ref, *refs, n_pages):
    k_refs, v_refs, f_ref = refs[:n_pages], refs[n_pages:2 * n_pages], refs[2 * n_pages]
    o_ref, qb_scr, m_scr, l_scr, c_scr, s_scr, p_scr, acc_scr = refs[2 * n_pages + 1:]
    gi = pl.program_id(1)
    n_heads = qb_scr.shape[0]
    first_page = (pl.num_programs(1) - 1 - gi) * n_pages

    def pair_column(row_ref, pair):
        return _row_to_col(row_ref[0][:, pair * LANES:(pair + 1) * LANES])

    @pl.when(gi == 0)
    def _():
        for pair in range(n_heads // 2):
            q_col = pair_column(q_ref, pair) * SCALE
            for e in range(2):
                qb_scr[2 * pair + e] = jnp.broadcast_to(q_col[e * HEAD_DIM:(e + 1) * HEAD_DIM], (HEAD_DIM, PAGE))
        m_scr[...] = jnp.full_like(m_scr, NEG)
        l_scr[...] = jnp.zeros_like(l_scr)
        acc_scr[...] = jnp.zeros_like(acc_scr)
        c_scr[...] = _row_to_col(fn_ref[0])

    r = lax.broadcasted_iota(jnp.int32, (PAGE, PAGE), 0)
    c = lax.broadcasted_iota(jnp.int32, (PAGE, PAGE), 1)
    later = jnp.where(r > c, 1.0, 0.0).astype(BF16)
    logf = jnp.concatenate([f_ref[pt_ref[pl.program_id(0), first_page + i]] for i in range(n_pages)], axis=0)
    within = _dot3_r(logf, later)
    page_total = within[:, 0:1] + logf[:, 0:1]
    carry = c_scr[...]
    after = [None] * n_pages
    for i in reversed(range(n_pages)):
        after[i] = carry
        carry = carry + page_total[i * n_heads:(i + 1) * n_heads]
        for h in range(n_heads):
            s_scr[pl.ds(i * n_heads + h, 1), :] = jnp.sum(k_refs[i][0, 0, h] * qb_scr[h], axis=0, keepdims=True)
    c_scr[...] = carry
    scores = s_scr[...] + within + jnp.concatenate(after, axis=0)

    m_old = m_scr[...]
    row_max = jnp.max(scores, axis=1, keepdims=True)
    m_new = m_old
    for i in range(n_pages):
        m_new = jnp.maximum(m_new, row_max[i * n_heads:(i + 1) * n_heads])
    alpha = jnp.exp(m_old - m_new)
    pr = jnp.exp(scores - jnp.concatenate([m_new] * n_pages, axis=0))
    p_scr[...] = pr
    row_sum = jnp.sum(pr, axis=1, keepdims=True)
    l = alpha * l_scr[...]
    for i in range(n_pages):
        l = l + row_sum[i * n_heads:(i + 1) * n_heads]
    m_scr[...] = m_new
    l_scr[...] = l
    for h in range(n_heads):
        acc = alpha[h:h + 1, :] * acc_scr[h]
        for i in range(n_pages):
            acc = acc + p_scr[pl.ds(i * n_heads + h, 1), :] * v_refs[i][0, 0, h]
        acc_scr[h] = acc

    @pl.when(gi == pl.num_programs(1) - 1)
    def _():
        for pair in range(n_heads // 2):
            q_col, k_col, v_col = (pair_column(r, pair) for r in (q_ref, kn_ref, vn_ref))
            outs = []
            for e in range(2):
                h = 2 * pair + e
                own = slice(e * HEAD_DIM, (e + 1) * HEAD_DIM)
                s_new = jnp.sum(q_col[own] * SCALE * k_col[own], axis=0, keepdims=True)
                m_h = m_new[h:h + 1, :]
                m_f = jnp.maximum(m_h, s_new)
                a_f = jnp.exp(m_h - m_f)
                p_new = jnp.exp(s_new - m_f)
                l_f = a_f * l[h:h + 1, :] + p_new
                past = jnp.sum(acc_scr[h], axis=1, keepdims=True)
                outs.append((a_f * past + p_new * v_col[own]) / l_f)
            o_ref[0, :, pair * LANES:(pair + 1) * LANES] = _col_to_row(jnp.concatenate(outs, axis=0))


def _sample_attn(q, k_new, v_new, logf_new, cache_k, cache_v, cache_logf, page_table, layer):
    n, width = q.shape
    n_heads = width // HEAD_DIM
    n_seq_pages = page_table.shape[1]
    pps = min(PAGES_PER_STEP, n_seq_pages)
    n_steps = n_seq_pages // pps
    kt = cache_k.transpose(0, 1, 3, 4, 2)
    vt = cache_v.transpose(0, 1, 3, 4, 2)
    ft = cache_logf.transpose(0, 1, 3, 2)

    def page(i, *tail):
        return pl.BlockSpec((1, 1) + tail,
                            lambda b, g, pt: (layer, pt[b, (n_steps - 1 - g) * pps + i]) + (0,) * len(tail))

    row = lambda last: pl.BlockSpec((1, 1, last), lambda b, g, pt: (b, 0, 0))
    grid_spec = pltpu.PrefetchScalarGridSpec(
        num_scalar_prefetch=1,
        grid=(n, n_steps),
        in_specs=[row(width), row(width), row(width), row(n_heads)]
                 + [page(i, n_heads, HEAD_DIM, PAGE) for i in range(pps)] * 2
                 + [pl.BlockSpec((None,) + ft.shape[1:], lambda b, g, pt: (layer, 0, 0, 0),
                                 pipeline_mode=pl.Buffered(1))],
        out_specs=row(width),
        scratch_shapes=[pltpu.VMEM((n_heads, HEAD_DIM, PAGE), F32),
                        pltpu.VMEM((n_heads, 1), F32), pltpu.VMEM((n_heads, 1), F32), pltpu.VMEM((n_heads, 1), F32),
                        pltpu.VMEM((pps * n_heads, PAGE), F32), pltpu.VMEM((pps * n_heads, PAGE), F32),
                        pltpu.VMEM((n_heads, HEAD_DIM, PAGE), F32)],
    )
    as_rows = lambda a: a.reshape(n, 1, a.shape[1])
    out = pl.pallas_call(
        functools.partial(_sample_attn_body, n_pages=pps),
        grid_spec=grid_spec,
        out_shape=jax.ShapeDtypeStruct((n, 1, width), F32),
        compiler_params=pltpu.CompilerParams(dimension_semantics=("arbitrary", "arbitrary"),
                                             vmem_limit_bytes=VMEM_LIMIT),
        name="sample_attn",
    )(page_table, as_rows(q), as_rows(k_new), as_rows(v_new), as_rows(logf_new),
      *([kt] * pps), *([vt] * pps), ft)
    return out.reshape(n, width)


def _sample_out_body(x_ref, yab_ref, yc_ref, mod_ref, n2g_ref, mixg_ref, wout_ref, wg_ref, wu_ref, wd_ref,
                     taps_g_ref, taps_u_ref, s0g_ref, s0u_ref, s1g_ref, s1u_ref, fg_ref,
                     o_ref, ng_ref, nu_ref, pg_ref, pu_ref, x1_scr, hb_scr, acc_scr, *, final):
    c = pl.program_id(0)
    d = x_ref.shape[1]
    ab_w = yab_ref.shape[1]

    @pl.when(c == 0)
    def _():
        y_c = (_rms(yc_ref[...]) * mixg_ref[:, ab_w:]).astype(BF16)
        attn = _dot(yab_ref[...], wout_ref[0:ab_w, :]) + _dot(y_c, wout_ref[ab_w:, :])
        x1 = x_ref[...] + mod_ref[:, 2 * d:3 * d] * attn
        x1_scr[...] = x1
        hb_scr[...] = _modulated_norm(x1, n2g_ref[...], mod_ref[:, 4 * d:5 * d], mod_ref[:, 3 * d:4 * d]).astype(BF16)
        acc_scr[...] = jnp.zeros_like(acc_scr)

    tg, tu = taps_g_ref[0], taps_u_ref[0]
    g, u, down = _ffn_chunk(hb_scr[...], wg_ref[...], wu_ref[...], wd_ref[...],
                            [tg[i:i + 1] for i in range(4)], [tu[i:i + 1] for i in range(4)],
                            (s0g_ref[...], s1g_ref[...]), (s0u_ref[...], s1u_ref[...]))
    ng_ref[...] = g
    nu_ref[...] = u
    pg_ref[...] = s1g_ref[...]
    pu_ref[...] = s1u_ref[...]
    acc_scr[...] += down

    @pl.when(c == pl.num_programs(0) - 1)
    def _():
        x2 = x1_scr[...] + mod_ref[:, 5 * d:6 * d] * acc_scr[...]
        o_ref[...] = _rms(x2) * fg_ref[...] if final else x2


def _sample_out(x, yab, yc, mod, state, p, final_g, final):
    n, d = x.shape
    n_chunks = p["w_up"].shape[2] // (2 * MXU_N)
    d_ff = n_chunks * MXU_N
    const = lambda *shape: pl.BlockSpec(shape, lambda c: (0,) * len(shape))
    st = lambda off: pl.BlockSpec((n, MXU_N), lambda c: (0, off + c))
    col = pl.BlockSpec((n, MXU_N), lambda c: (0, c))
    outs = pl.pallas_call(
        functools.partial(_sample_out_body, final=final),
        grid=(n_chunks,),
        in_specs=[const(n, d), const(*yab.shape), const(*yc.shape), const(*mod.shape), const(1, d), const(1, d),
                  _layer_spec(p["w_out"], p["layer"]),
                  pl.BlockSpec((None, d, MXU_N), lambda c: (p["layer"], 0, c)),
                  pl.BlockSpec((None, d, MXU_N), lambda c: (p["layer"], 0, n_chunks + c)),
                  pl.BlockSpec((None, MXU_N, d), lambda c: (p["layer"], c, 0)),
                  pl.BlockSpec((1, SUBLANES, MXU_N), lambda c: (c, 0, 0)),
                  pl.BlockSpec((1, SUBLANES, MXU_N), lambda c: (n_chunks + c, 0, 0)),
                  st(0), st(n_chunks), st(2 * n_chunks), st(3 * n_chunks), const(1, d)],
        out_specs=[const(n, d), col, col, col, col],
        out_shape=[jax.ShapeDtypeStruct((n, d), F32)] + [jax.ShapeDtypeStruct((n, d_ff), F32)] * 4,
        scratch_shapes=[pltpu.VMEM((n, d), F32), pltpu.VMEM((n, d), BF16), pltpu.VMEM((n, d), F32)],
        compiler_params=pltpu.CompilerParams(dimension_semantics=("arbitrary",), vmem_limit_bytes=VMEM_LIMIT),
        name="sample_out",
    )(x, yab, yc, mod, p["norm2_g"], p["mix_g"], p["w_out"], p["w_up"], p["w_up"], p["w_down_full"],
      p["ffn_taps"], p["ffn_taps"], state, state, state, state, final_g)
    x2, new_g, new_u, prev_g, prev_u = outs
    new_state = jnp.stack([jnp.concatenate([prev_g, prev_u], axis=1), jnp.concatenate([new_g, new_u], axis=1)], axis=1)
    return x2, new_state


def _mxu_weights(w_in, w_out, w_up, w_down, main, c_w):
    return dict(w_in=w_in[:, :, :main].astype(BF16),
                w_vt=w_in[:, :, main - c_w:main].transpose(0, 2, 1).astype(BF16),
                w_out=w_out.astype(BF16), w_up=w_up.astype(BF16), w_down_full=w_down.astype(BF16))


def _layer_params(l, d, mxu, norm1_g, norm2_g, w_in, b_forget, a_ln_g, a_ln_b, w_s, b_s, conv_w, conv_b,
                  conv_ln_g, conv_ln_b, mix_g, ffn_conv_w, ffn_conv_b):
    a_w, b_w, n_heads = a_ln_g.shape[1], conv_b.shape[1], b_forget.shape[1]
    c_w = n_heads * HEAD_DIM
    main = 2 * a_w + 2 * b_w + 3 * c_w
    d_ff = mxu["w_down_full"].shape[1]
    n_chunks = d_ff // MXU_N
    row = lambda v: v.reshape(1, -1)
    pad_lanes = lambda a: jnp.pad(a, ((0, 0), (0, LANES - a.shape[1])))
    decay_q, decay_k = _decay_placement(n_heads)
    chunked = lambda v: v.reshape(2 * n_chunks, MXU_N)
    taps = jnp.stack([chunked(ffn_conv_w[l, 0]), chunked(ffn_conv_w[l, 1]), chunked(ffn_conv_w[l, 2]),
                      chunked(ffn_conv_b[l])], axis=1)
    taps = jnp.pad(taps, ((0, 0), (0, SUBLANES - 4), (0, 0)))
    return dict(
        mxu, layer=l,
        a_w=a_w, b_w=b_w, c_w=c_w,
        norm1_g=row(norm1_g[l]), norm2_g=row(norm2_g[l]),
        w_f=pad_lanes(jnp.tile(w_in[l, :, main:], (1, 3))).astype(BF16),
        b_f=pad_lanes(jnp.tile(row(b_forget[l]), (1, 3))),
        decay_q=decay_q, decay_k=decay_k,
        a_ln_g=row(a_ln_g[l]), a_ln_b=row(a_ln_b[l]),
        w_s=w_s[l], b_s_full=jnp.repeat(b_s[l].T, HEAD_DIM, axis=1),
        w_s0=row(jnp.repeat(w_s[l, :, 0, 0], HEAD_DIM)), b_s0=row(jnp.repeat(b_s[l, :, 0], HEAD_DIM)),
        conv_w=conv_w[l], conv_b=row(conv_b[l]), conv_ln_g=row(conv_ln_g[l]), conv_ln_b=row(conv_ln_b[l]),
        mix_g=row(mix_g[l]),
        ffn_taps=taps,
    )


def kernel(x_prompt, x_sample, cache_k, cache_v, cache_logf, state_conv, state_ffn_conv, page_table, c_prompt, c_sample, norm1_g, norm2_g, w_ada, b_ada, w_in, b_forget, a_ln_g, a_ln_b, w_s, b_s, conv_w, conv_b, conv_ln_g, conv_ln_b, mix_g, w_out, w_up, ffn_conv_w, ffn_conv_b, w_down, final_g):
    nb, t, d = x_prompt.shape
    ns = x_sample.shape[0]
    depth = w_in.shape[0]
    n_heads = b_forget.shape[1]
    tm = min(ROW_TILE, t)

    c_all = jnp.concatenate([c_prompt, c_sample], axis=0)
    pad_rows = (-c_all.shape[0]) % 16
    mod = _ada(jnp.pad(c_all, ((0, pad_rows), (0, 0))), w_ada, b_ada)
    fg = final_g.reshape(1, d)

    c_w = n_heads * HEAD_DIM
    mxu = _mxu_weights(w_in, w_out, w_up, w_down, w_in.shape[2] - n_heads, c_w)
    xp, xs = x_prompt, x_sample.reshape(ns, d)
    prompt_out, sample_out = [], []
    for l in range(depth):
        p = _layer_params(l, d, mxu, norm1_g, norm2_g, w_in, b_forget, a_ln_g, a_ln_b, w_s, b_s, conv_w, conv_b,
                          conv_ln_g, conv_ln_b, mix_g, ffn_conv_w, ffn_conv_b)
        n_chunks = p["w_up"].shape[2] // (2 * MXU_N)
        final = l == depth - 1
        mod_p = mod[l, :nb].reshape(nb, 6, d)
        mod_s = mod[l, nb:nb + ns]

        k, vt, logf, qh, kh, vh, yab, conv_st = _prompt_in(xp, mod_p, p, tm)
        yc = _prompt_attn(qh, kh, vh, tm)
        xp, ffn_tail = _prompt_out(xp, yab, yc, mod_p, p, fg, tm, final)
        ffn_st = ffn_tail[:, :, SUBLANES - 2:, :].transpose(0, 2, 1, 3).reshape(nb, 2, 2 * n_chunks * MXU_N)
        v_rows = vt.reshape(nb, n_heads, HEAD_DIM, t).transpose(0, 3, 1, 2)
        prompt_out.append((k.reshape(nb, t, n_heads, HEAD_DIM), v_rows, logf, conv_st, ffn_st))

        st_conv = state_conv[l].reshape(ns, -1)
        q_s, k_s, v_s, logf_s, yab_s, chunk_v, new_conv = _sample_in(xs, mod_s, st_conv, p)
        yc_s = _sample_attn(q_s, k_s, v_s, logf_s, cache_k, cache_v, cache_logf, page_table, l)
        xs, new_ffn = _sample_out(xs, yab_s, yc_s, mod_s, state_ffn_conv[l].reshape(ns, -1), p, fg, final)
        sample_out.append((k_s.reshape(ns, 1, n_heads, HEAD_DIM), v_s.reshape(ns, 1, n_heads, HEAD_DIM),
                           logf_s.reshape(ns, 1, n_heads), new_conv.reshape(state_conv.shape[1:]), new_ffn,
                           chunk_v.reshape(ns, 1, -1)))

    stack = lambda outs, i: jnp.stack([o[i] for o in outs])
    return (xp, xs.reshape(ns, 1, d),
            stack(prompt_out, 0), stack(prompt_out, 1), stack(prompt_out, 2), stack(prompt_out, 3), stack(prompt_out, 4),
            stack(sample_out, 0), stack(sample_out, 1), stack(sample_out, 2), stack(sample_out, 3), stack(sample_out, 4),
            stack(sample_out, 5))
```
